```python
import jax, jax.numpy as jnp
from jax import lax
import numpy as np

D_MODEL = 1024
BATCH = 8
SEQ = 2048
DEPTH = 1

D_MIX = D_MODEL
D_CONV = D_MIX // 2
D_ATTN = D_MIX - D_CONV
CONV_WIDTH = 3
ATTN_HEADS = 8
HEAD_DIM = D_ATTN // ATTN_HEADS
MOBA_BLOCK = 256
MOBA_TOPK = 3
Q_CHUNK = 128
D_PROJ = 3 * D_CONV + 3 * D_ATTN
PEER_HEADS = 8
PEER_NKEYS = 128
PEER_N = PEER_NKEYS * PEER_NKEYS
PEER_DKEY = 256
PEER_TOPK = 16
PEER_CHUNK = 128
EPS = 1e-6
NEG = -1e30

kernel_name = "hymba_conv_moba_peer_block"


def rmsnorm(x, g):
    xf = x.astype(jnp.float32)
    y = xf * lax.rsqrt(jnp.mean(xf * xf, axis=-1, keepdims=True) + EPS)
    return (y * g.astype(jnp.float32)).astype(x.dtype)


def short_conv_mixer(b_gate, c_gate, h, conv_w):
    u = c_gate * h
    y = lax.conv_general_dilated(
        u, conv_w[:, None, :].astype(u.dtype), window_strides=(1,),
        padding=[(CONV_WIDTH - 1, 0)], dimension_numbers=("NWC", "WIO", "NWC"),
        feature_group_count=D_CONV)
    return b_gate * y


def moba_attention(q, k, v):
    b_, s_, h_, hd = q.shape
    nb = -(-s_ // MOBA_BLOCK)
    pad = nb * MOBA_BLOCK - s_
    kp = jnp.pad(k, ((0, 0), (0, pad), (0, 0), (0, 0)))
    vp = jnp.pad(v, ((0, 0), (0, pad), (0, 0), (0, 0)))
    kb = kp.reshape(b_, nb, MOBA_BLOCK, h_, hd).transpose(0, 3, 1, 2, 4)
    vb = vp.reshape(b_, nb, MOBA_BLOCK, h_, hd).transpose(0, 3, 1, 2, 4)
    k_mean = jnp.mean(kb.astype(jnp.float32), axis=3)
    qh = q.transpose(0, 2, 1, 3)
    gate = jnp.einsum("bhsd,bhnd->bhsn", qh.astype(jnp.float32), k_mean)
    own = jnp.arange(s_, dtype=jnp.int32) // MOBA_BLOCK
    past = jnp.arange(nb, dtype=jnp.int32)[None, :] < own[:, None]
    gate = jnp.where(past[None, None], gate, NEG)
    k_sel = min(MOBA_TOPK, nb)
    _, sel = lax.top_k(gate, k_sel)
    valid = sel < own[None, None, :, None]

    nc = s_ // Q_CHUNK
    q_c = qh.reshape(b_, h_, nc, Q_CHUNK, hd).transpose(0, 2, 1, 3, 4).reshape(b_ * nc, h_, Q_CHUNK, hd)
    sel_c = sel.reshape(b_, h_, nc, Q_CHUNK, k_sel).transpose(0, 2, 1, 3, 4).reshape(b_ * nc, h_, Q_CHUNK, k_sel)
    val_c = valid.reshape(b_, h_, nc, Q_CHUNK, k_sel).transpose(0, 2, 1, 3, 4).reshape(b_ * nc, h_, Q_CHUNK, k_sel)
    b_idx = jnp.repeat(jnp.arange(b_, dtype=jnp.int32), nc)
    c_idx = jnp.tile(jnp.arange(nc, dtype=jnp.int32), b_)
    scale = 1.0 / math_sqrt(hd)

    def step(args):
        qc, selc, validc, bi, ci = args
        kbb = kb[bi]
        vbb = vb[bi]
        k_g = jax.vmap(lambda kh, sh: kh[sh])(kbb, selc)
        v_g = jax.vmap(lambda vh, sh: vh[sh])(vbb, selc)
        q_pos = ci * Q_CHUNK + jnp.arange(Q_CHUNK, dtype=jnp.int32)
        own_b = (ci * Q_CHUNK) // MOBA_BLOCK
        k_own = lax.dynamic_index_in_dim(kbb, own_b, axis=1, keepdims=False)
        v_own = lax.dynamic_index_in_dim(vbb, own_b, axis=1, keepdims=False)
        s_sel = jnp.einsum("hqd,hqkjd->hqkj", qc, k_g).astype(jnp.float32) * scale
        s_sel = jnp.where(validc[..., None], s_sel, NEG).reshape(h_, Q_CHUNK, k_sel * MOBA_BLOCK)
        s_own = jnp.einsum("hqd,hjd->hqj", qc, k_own).astype(jnp.float32) * scale
        k_pos = own_b * MOBA_BLOCK + jnp.arange(MOBA_BLOCK, dtype=jnp.int32)
        s_own = jnp.where(k_pos[None, None, :] <= q_pos[None, :, None], s_own, NEG)
        p = jax.nn.softmax(jnp.concatenate([s_sel, s_own], axis=-1), axis=-1)
        p_sel = p[..., : k_sel * MOBA_BLOCK].reshape(h_, Q_CHUNK, k_sel, MOBA_BLOCK).astype(v.dtype)
        p_own = p[..., k_sel * MOBA_BLOCK:].astype(v.dtype)
        return (jnp.einsum("hqkj,hqkjd->hqd", p_sel, v_g)
                + jnp.einsum("hqj,hjd->hqd", p_own, v_own))

    o = lax.map(step, (q_c, sel_c, val_c, b_idx, c_idx))
    o = o.reshape(b_, nc, h_, Q_CHUNK, hd).transpose(0, 1, 3, 2, 4)
    return o.reshape(b_, s_, h_ * hd)


def math_sqrt(n):
    return float(np.sqrt(n))


def peer_ffn(x, w_query, sub_keys1, sub_keys2, expert_down, expert_up):
    b_, s_, d_ = x.shape
    t = b_ * s_
    half = PEER_DKEY // 2

    def step(xc):
        qry = (xc @ w_query).reshape(PEER_CHUNK, PEER_HEADS, PEER_DKEY)
        s1 = jnp.einsum("chd,hnd->chn", qry[..., :half], sub_keys1).astype(jnp.float32)
        s2 = jnp.einsum("chd,hnd->chn", qry[..., half:], sub_keys2).astype(jnp.float32)
        v1, i1 = lax.top_k(s1, PEER_TOPK)
        v2, i2 = lax.top_k(s2, PEER_TOPK)
        cand = (v1[..., :, None] + v2[..., None, :]).reshape(PEER_CHUNK, PEER_HEADS, PEER_TOPK * PEER_TOPK)
        cidx = (i1[..., :, None] * PEER_NKEYS + i2[..., None, :]).reshape(PEER_CHUNK, PEER_HEADS, PEER_TOPK * PEER_TOPK)
        top_s, pos = lax.top_k(cand, PEER_TOPK)
        eidx = jnp.take_along_axis(cidx, pos, axis=-1)
        g = jax.nn.softmax(top_s, axis=-1).astype(xc.dtype)
        u = expert_down[eidx]
        act = jax.nn.gelu(jnp.einsum("chkd,cd->chk", u, xc), approximate=False)
        vv = expert_up[eidx]
        return jnp.einsum("chk,chkd->cd", g * act, vv)

    out = lax.map(step, x.reshape(t // PEER_CHUNK, PEER_CHUNK, d_))
    return out.reshape(b_, s_, d_)


def setup_inputs(seed: int = 0) -> dict:
    key = jax.random.key(seed)
    ks = jax.random.split(key, 16)
    f32 = jnp.float32
    nrm = lambda k, shape, sc: jax.random.normal(k, shape, f32) * sc
    gain = lambda k, n: 1.0 + 0.02 * jax.random.normal(k, (DEPTH, n), f32)
    return {
        "x": jax.random.normal(ks[0], (BATCH, SEQ, D_MODEL), f32),
        "norm_mix_g": gain(ks[1], D_MODEL),
        "w_in": nrm(ks[2], (DEPTH, D_MODEL, D_PROJ), D_MODEL ** -0.5),
        "conv_w": nrm(ks[3], (DEPTH, CONV_WIDTH, D_CONV), CONV_WIDTH ** -0.5),
        "norm_conv_out_g": gain(ks[4], D_CONV),
        "norm_attn_out_g": gain(ks[5], D_ATTN),
        "w_out": nrm(ks[6], (DEPTH, D_MIX, D_MODEL), D_MIX ** -0.5),
        "norm_ffn_g": gain(ks[7], D_MODEL),
        "peer_w_query": nrm(ks[8], (DEPTH, D_MODEL, PEER_HEADS * PEER_DKEY), D_MODEL ** -0.5),
        "peer_sub_keys1": nrm(ks[9], (DEPTH, PEER_HEADS, PEER_NKEYS, PEER_DKEY // 2), (PEER_DKEY // 2) ** -0.5),
        "peer_sub_keys2": nrm(ks[10], (DEPTH, PEER_HEADS, PEER_NKEYS, PEER_DKEY // 2), (PEER_DKEY // 2) ** -0.5),
        "peer_down": nrm(ks[11], (DEPTH, PEER_N, D_MODEL), D_MODEL ** -0.5),
        "peer_up": nrm(ks[12], (DEPTH, PEER_N, D_MODEL), 0.5),
        "final_norm_g": 1.0 + 0.02 * jax.random.normal(ks[13], (D_MODEL,), f32),
    }


def reference(x, norm_mix_g, w_in, conv_w, norm_conv_out_g, norm_attn_out_g, w_out,
              norm_ffn_g, peer_w_query, peer_sub_keys1, peer_sub_keys2, peer_down,
              peer_up, final_norm_g):
    b_, s_, _ = x.shape
    for l in range(DEPTH):
        h = rmsnorm(x, norm_mix_g[l])
        proj = h @ w_in[l]
        b_gate, c_gate, h_c, q, k, v = jnp.split(
            proj, [D_CONV, 2 * D_CONV, 3 * D_CONV, 3 * D_CONV + D_ATTN, 3 * D_CONV + 2 * D_ATTN], axis=-1)
        y_conv = short_conv_mixer(b_gate, c_gate, h_c, conv_w[l])
        y_attn = moba_attention(q.reshape(b_, s_, ATTN_HEADS, HEAD_DIM),
                                k.reshape(b_, s_, ATTN_HEADS, HEAD_DIM),
                                v.reshape(b_, s_, ATTN_HEADS, HEAD_DIM))
        mixed = jnp.concatenate([rmsnorm(y_conv, norm_conv_out_g[l]),
                                 rmsnorm(y_attn, norm_attn_out_g[l])], axis=-1)
        x = x + mixed @ w_out[l]
        x = x + peer_ffn(rmsnorm(x, norm_ffn_g[l]), peer_w_query[l], peer_sub_keys1[l],
                         peer_sub_keys2[l], peer_down[l], peer_up[l])
    return rmsnorm(x, final_norm_g)
```

```python
import functools

import numpy as np
import jax
import jax.numpy as jnp
from jax import lax
from jax.experimental import pallas as pl
from jax.experimental.pallas import tpu as pltpu

F32 = jnp.float32
BF16 = jnp.bfloat16

D_MODEL = 1024
D_CONV = 512
D_ATTN = 512
D_PROJ = 3 * D_CONV + 3 * D_ATTN
CONV_WIDTH = 3
ATTN_HEADS = 8
HEAD_DIM = 64
MOBA_BLOCK = 256
MOBA_TOPK = 3
PEER_HEADS = 8
PEER_NKEYS = 128
PEER_N = PEER_NKEYS * PEER_NKEYS
PEER_DKEY = 256
PEER_HALF = PEER_DKEY // 2
PEER_TOPK = 16
EPS = 1e-6
NEG = -1e30

VMEM_LIMIT_BYTES = 48 * 1024 * 1024
SUBLANES = 8

TM_INPROJ = 512
TM_MIX = 512
TM_PEER = 256
PEER_ROWS_PER_STEP = 4
PEER_EB = PEER_ROWS_PER_STEP * PEER_NKEYS
TM_FINAL = 512

_CAND_PAIRS = [(r1, r2) for r1 in range(PEER_TOPK) for r2 in range(PEER_TOPK)
               if (r1 + 1) * (r2 + 1) <= PEER_TOPK]
_CAND_ROWS = -(-len(_CAND_PAIRS) // SUBLANES) * SUBLANES


def _params(*sem):
    return pltpu.CompilerParams(dimension_semantics=sem, vmem_limit_bytes=VMEM_LIMIT_BYTES)


def _rms(x, g):
    return x * lax.rsqrt(jnp.mean(x * x, axis=-1, keepdims=True) + EPS) * g


def _inproj_body(x_ref, g_ref, w_ref, b_ref, u_ref, q_ref, k_ref, v_ref):
    h = _rms(x_ref[...], g_ref[...]).astype(BF16)
    proj = jnp.dot(h, w_ref[...], preferred_element_type=F32)
    b_ref[...] = proj[:, 0:D_CONV]
    u_ref[...] = proj[:, D_CONV:2 * D_CONV] * proj[:, 2 * D_CONV:3 * D_CONV]
    o = 3 * D_CONV
    q_ref[...] = proj[:, o:o + D_ATTN].astype(BF16)
    k_ref[...] = proj[:, o + D_ATTN:o + 2 * D_ATTN].astype(BF16)
    v_ref[...] = proj[:, o + 2 * D_ATTN:o + 3 * D_ATTN].astype(BF16)


def _inproj(x2d, g, w_in_b):
    t = x2d.shape[0]
    tm = TM_INPROJ
    row = lambda i: (i, 0)
    fixed = lambda i: (0, 0)
    return pl.pallas_call(
        _inproj_body,
        grid=(t // tm,),
        in_specs=[pl.BlockSpec((tm, D_MODEL), row),
                  pl.BlockSpec((1, D_MODEL), fixed),
                  pl.BlockSpec((D_MODEL, D_PROJ), fixed)],
        out_specs=[pl.BlockSpec((tm, D_CONV), row)] * 2 + [pl.BlockSpec((tm, D_ATTN), row)] * 3,
        out_shape=[jax.ShapeDtypeStruct((t, D_CONV), F32)] * 2
                  + [jax.ShapeDtypeStruct((t, D_ATTN), BF16)] * 3,
        compiler_params=_params("parallel"),
        name="inproj",
    )(x2d, g, w_in_b)


def _moba_body(qT_ref, k_ref, vT_ref, o_ref, *, seq):
    nb = seq // MOBA_BLOCK
    bs = MOBA_BLOCK
    scale = float(1.0 / np.sqrt(HEAD_DIM))
    qT = qT_ref[...]
    blk = lax.broadcasted_iota(jnp.int32, (nb, seq), 0)
    own = lax.shift_right_logical(lax.broadcasted_iota(jnp.int32, (nb, seq), 1),
                                  int(np.log2(bs)))
    avg = jnp.where(own == blk, 1.0 / bs, 0.0).astype(BF16)
    kmean = jnp.dot(avg, k_ref[...], preferred_element_type=F32)
    gate = jnp.dot(kmean.astype(BF16), qT, preferred_element_type=F32)
    past = blk < own
    g1 = jnp.where(past, gate, NEG)
    thr = g1
    for _ in range(MOBA_TOPK - 1):
        m = jnp.max(thr, axis=0, keepdims=True)
        thr = jnp.where(thr == m, NEG, thr)
    thr = jnp.max(thr, axis=0, keepdims=True)
    sel = jnp.where(jnp.logical_and(past, g1 >= thr), 1.0, 0.0)

    kpos = lax.broadcasted_iota(jnp.int32, (bs, bs), 0)
    qpos = lax.broadcasted_iota(jnp.int32, (bs, bs), 1)
    causal = kpos <= qpos
    for i in range(nb):
        n = (i + 1) * bs
        lanes = slice(i * bs, (i + 1) * bs)
        sT = jnp.dot(k_ref[0:n, :], qT[:, lanes], preferred_element_type=F32) * scale
        pieces = []
        for j in range(i):
            keep = sel[j:j + 1, lanes] > 0.5
            pieces.append(jnp.where(keep, sT[j * bs:(j + 1) * bs], NEG))
        pieces.append(jnp.where(causal, sT[i * bs:n], NEG))
        s = pieces[0] if i == 0 else jnp.concatenate(pieces, axis=0)
        m = jnp.max(s, axis=0, keepdims=True)
        p = jnp.exp(s - m)
        l = jnp.sum(p, axis=0, keepdims=True)
        oT = jnp.dot(vT_ref[:, 0:n], p.astype(BF16), preferred_element_type=F32)
        o_ref[:, lanes] = oT / l


def _moba(qT, k, vT):
    b, h, hd, s = qT.shape
    return pl.pallas_call(
        functools.partial(_moba_body, seq=s),
        grid=(b, h),
        in_specs=[pl.BlockSpec((None, None, hd, s), lambda i, j: (i, j, 0, 0)),
                  pl.BlockSpec((None, None, s, hd), lambda i, j: (i, j, 0, 0)),
                  pl.BlockSpec((None, None, hd, s), lambda i, j: (i, j, 0, 0))],
        out_specs=pl.BlockSpec((None, None, hd, s), lambda i, j: (i, j, 0, 0)),
        out_shape=jax.ShapeDtypeStruct((b, h, hd, s), F32),
        compiler_params=_params("parallel", "parallel"),
        name="moba",
    )(qT, k, vT)


def _mix_body(x_ref, b_ref, u_ref, uh_ref, ya_ref, cw_ref, gc_ref, ga_ref, wo_ref, gf_ref,
              x2_ref, xn_ref, uext_ref, *, tiles_per_seq):
    tm = u_ref.shape[0]
    first = (pl.program_id(0) % tiles_per_seq) == 0
    uext_ref[0:SUBLANES, :] = jnp.where(first, 0.0, uh_ref[...])
    u = u_ref[...]
    uext_ref[SUBLANES:SUBLANES + tm, :] = u
    cw = cw_ref[...]
    y = (cw[2:3, :] * u
         + cw[1:2, :] * uext_ref[SUBLANES - 1:SUBLANES - 1 + tm, :]
         + cw[0:1, :] * uext_ref[SUBLANES - 2:SUBLANES - 2 + tm, :])
    rc = _rms(b_ref[...] * y, gc_ref[...]).astype(BF16)
    ra = _rms(ya_ref[...], ga_ref[...]).astype(BF16)
    mix = (jnp.dot(rc, wo_ref[0:D_CONV, :], preferred_element_type=F32)
           + jnp.dot(ra, wo_ref[D_CONV:D_CONV + D_ATTN, :], preferred_element_type=F32))
    x2 = x_ref[...] + mix
    x2_ref[...] = x2
    xn_ref[...] = _rms(x2, gf_ref[...]).astype(BF16)


def _mix(x2d, bgate, u, yattn, conv_w, gc, ga, w_out_b, gf, seq):
    t = x2d.shape[0]
    tm = TM_MIX
    row = lambda i: (i, 0)
    fixed = lambda i: (0, 0)
    halo = lambda i: (jnp.maximum(i * (tm // SUBLANES) - 1, 0), 0)
    return pl.pallas_call(
        functools.partial(_mix_body, tiles_per_seq=seq // tm),
        grid=(t // tm,),
        in_specs=[pl.BlockSpec((tm, D_MODEL), row),
                  pl.BlockSpec((tm, D_CONV), row),
                  pl.BlockSpec((tm, D_CONV), row),
                  pl.BlockSpec((SUBLANES, D_CONV), halo),
                  pl.BlockSpec((tm, D_ATTN), row),
                  pl.BlockSpec((CONV_WIDTH, D_CONV), fixed),
                  pl.BlockSpec((1, D_CONV), fixed),
                  pl.BlockSpec((1, D_ATTN), fixed),
                  pl.BlockSpec((D_MODEL, D_MODEL), fixed),
                  pl.BlockSpec((1, D_MODEL), fixed)],
        out_specs=[pl.BlockSpec((tm, D_MODEL), row), pl.BlockSpec((tm, D_MODEL), row)],
        out_shape=[jax.ShapeDtypeStruct((t, D_MODEL), F32),
                   jax.ShapeDtypeStruct((t, D_MODEL), BF16)],
        scratch_shapes=[pltpu.VMEM((tm + SUBLANES, D_CONV), F32)],
        compiler_params=_params("parallel"),
        name="mix",
    )(x2d, bgate, u, u, yattn, conv_w, gc, ga, w_out_b, gf)


def _top_desc(s, n, dst_ref):
    for r in range(n):
        m = jnp.max(s, axis=0, keepdims=True)
        dst_ref[r:r + 1, :] = m
        s = jnp.where(s == m, NEG, s)


def _prep_body(xnT_ref, wqT_ref, k1_ref, k2_ref, rk_ref, cn_ref, e1_ref, e2_ref,
               qry_ref, v1_ref, v2_ref, cand_ref, top_ref):
    qry_ref[...] = jnp.dot(wqT_ref[...], xnT_ref[...], preferred_element_type=F32)
    last = PEER_TOPK - 1
    for h in range(PEER_HEADS):
        o = h * PEER_DKEY
        q1 = qry_ref[o:o + PEER_HALF, :].astype(BF16)
        q2 = qry_ref[o + PEER_HALF:o + PEER_DKEY, :].astype(BF16)
        s1 = jnp.dot(k1_ref[h], q1, preferred_element_type=F32)
        s2 = jnp.dot(k2_ref[h], q2, preferred_element_type=F32)
        _top_desc(s1, PEER_TOPK, v1_ref)
        _top_desc(s2, PEER_TOPK, v2_ref)
        for idx, (r1, r2) in enumerate(_CAND_PAIRS):
            cand_ref[idx:idx + 1, :] = v1_ref[r1:r1 + 1, :] + v2_ref[r2:r2 + 1, :]
        pad = _CAND_ROWS - len(_CAND_PAIRS)
        if pad:
            cand_ref[len(_CAND_PAIRS):_CAND_ROWS, :] = jnp.full((pad, cand_ref.shape[1]), NEG, F32)
        _top_desc(cand_ref[...], PEER_TOPK, top_ref)
        tops = top_ref[...]
        tau = tops[last:last + 1, :]
        z = jnp.sum(jnp.exp(tops - tops[0:1, :]), axis=0, keepdims=True)
        in1 = s1 >= v1_ref[last:last + 1, :]
        in2 = s2 >= v2_ref[last:last + 1, :]
        cnt = jnp.zeros_like(s1)
        rank2 = jnp.zeros_like(s2)
        for r in range(PEER_TOPK):
            v2r = v2_ref[r:r + 1, :]
            cnt = cnt + jnp.where(s1 + v2r >= tau, 1.0, 0.0)
            rank2 = rank2 + jnp.where(v2r > s2, 1.0, 0.0)
        rk_ref[h] = rank2
        cn_ref[h] = jnp.where(in1, cnt, 0.0)
        e1_ref[h] = jnp.where(in1, jnp.exp(s1 - v1_ref[0:1, :]), 0.0) / z
        e2_ref[h] = jnp.where(in2, jnp.exp(s2 - v2_ref[0:1, :]), 0.0)


def _prep(xnT, wqT_b, k1_b, k2_b):
    t = xnT.shape[1]
    tm = TM_PEER
    col = lambda i: (0, i)
    col3 = lambda i: (0, 0, i)
    hk = (PEER_HEADS, PEER_NKEYS, tm)
    return pl.pallas_call(
        _prep_body,
        grid=(t // tm,),
        in_specs=[pl.BlockSpec((D_MODEL, tm), col),
                  pl.BlockSpec((PEER_HEADS * PEER_DKEY, D_MODEL), lambda i: (0, 0)),
                  pl.BlockSpec((PEER_HEADS, PEER_NKEYS, PEER_HALF), lambda i: (0, 0, 0)),
                  pl.BlockSpec((PEER_HEADS, PEER_NKEYS, PEER_HALF), lambda i: (0, 0, 0))],
        out_specs=[pl.BlockSpec(hk, col3)] * 4,
        out_shape=[jax.ShapeDtypeStruct((PEER_HEADS, PEER_NKEYS, t), F32)] * 4,
        scratch_shapes=[pltpu.VMEM((PEER_HEADS * PEER_DKEY, tm), F32),
                        pltpu.VMEM((PEER_TOPK, tm), F32),
                        pltpu.VMEM((PEER_TOPK, tm), F32),
                        pltpu.VMEM((_CAND_ROWS, tm), F32),
                        pltpu.VMEM((PEER_TOPK, tm), F32)],
        compiler_params=_params("parallel"),
        name="prep",
    )(xnT, wqT_b, k1_b, k2_b)


def _peer_body(xnT_ref, dn_ref, upT_ref, rk_ref, cn_ref, e1_ref, e2_ref, o_ref, a_ref):
    j = pl.program_id(1)

    @pl.when(j == 0)
    def _():
        o_ref[...] = jnp.zeros_like(o_ref)

    hT = jnp.dot(dn_ref[...], xnT_ref[...], preferred_element_type=F32)
    sqrt_half = float(np.sqrt(0.5))
    for r in range(PEER_ROWS_PER_STEP):
        i1 = j * PEER_ROWS_PER_STEP + r
        w = None
        for h in range(PEER_HEADS):
            crow = cn_ref[h, pl.ds(i1, 1), :]
            erow = e1_ref[h, pl.ds(i1, 1), :]
            term = jnp.where(rk_ref[h] < crow, e2_ref[h] * erow, 0.0)
            w = term if w is None else w + term
        hr = hT[r * PEER_NKEYS:(r + 1) * PEER_NKEYS, :]
        act = 0.5 * hr * (1.0 + lax.erf(hr * sqrt_half))
        a_ref[r * PEER_NKEYS:(r + 1) * PEER_NKEYS, :] = (w * act).astype(BF16)
    o_ref[...] += jnp.dot(upT_ref[...], a_ref[...], preferred_element_type=F32)


def _peer(xnT, down_b, upT_b, rk, cn, e1, e2):
    t = xnT.shape[1]
    tm = TM_PEER
    hk = (PEER_HEADS, PEER_NKEYS, tm)
    tok3 = lambda i, j: (0, 0, i)
    return pl.pallas_call(
        _peer_body,
        grid=(t // tm, PEER_N // PEER_EB),
        in_specs=[pl.BlockSpec((D_MODEL, tm), lambda i, j: (0, i)),
                  pl.BlockSpec((PEER_EB, D_MODEL), lambda i, j: (j, 0)),
                  pl.BlockSpec((D_MODEL, PEER_EB), lambda i, j: (0, j)),
                  pl.BlockSpec(hk, tok3), pl.BlockSpec(hk, tok3),
                  pl.BlockSpec(hk, tok3), pl.BlockSpec(hk, tok3)],
        out_specs=pl.BlockSpec((D_MODEL, tm), lambda i, j: (0, i)),
        out_shape=jax.ShapeDtypeStruct((D_MODEL, t), F32),
        scratch_shapes=[pltpu.VMEM((PEER_EB, tm), BF16)],
        compiler_params=_params("parallel", "arbitrary"),
        name="peer",
    )(xnT, down_b, upT_b, rk, cn, e1, e2)


def _final_body(x2_ref, p_ref, g_ref, o_ref):
    o_ref[...] = _rms(x2_ref[...] + p_ref[...], g_ref[...])


def _final(x2, peer, g):
    t = x2.shape[0]
    tm = TM_FINAL
    row = lambda i: (i, 0)
    return pl.pallas_call(
        _final_body,
        grid=(t // tm,),
        in_specs=[pl.BlockSpec((tm, D_MODEL), row), pl.BlockSpec((tm, D_MODEL), row),
                  pl.BlockSpec((1, D_MODEL), lambda i: (0, 0))],
        out_specs=pl.BlockSpec((tm, D_MODEL), row),
        out_shape=jax.ShapeDtypeStruct((t, D_MODEL), F32),
        compiler_params=_params("parallel"),
        name="final",
    )(x2, peer, g)


def _layer(x2d, batch, seq, norm_mix_g, w_in, conv_w, norm_conv_out_g, norm_attn_out_g, w_out,
           norm_ffn_g, peer_w_query, peer_sub_keys1, peer_sub_keys2, peer_down, peer_up):
    t = batch * seq
    bgate, u, q, k, v = _inproj(x2d, norm_mix_g.reshape(1, D_MODEL), w_in.astype(BF16))
    heads = lambda a: a.reshape(batch, seq, ATTN_HEADS, HEAD_DIM)
    qT = heads(q).transpose(0, 2, 3, 1)
    kh = heads(k).transpose(0, 2, 1, 3)
    vT = heads(v).transpose(0, 2, 3, 1)
    yT = _moba(qT, kh, vT)
    yattn = yT.transpose(0, 3, 1, 2).reshape(t, D_ATTN)
    x2, xn = _mix(x2d, bgate, u, yattn, conv_w, norm_conv_out_g.reshape(1, D_CONV),
                  norm_attn_out_g.reshape(1, D_ATTN), w_out.astype(BF16),
                  norm_ffn_g.reshape(1, D_MODEL), seq)
    xnT = xn.T
    rk, cn, e1, e2 = _prep(xnT, peer_w_query.T.astype(BF16),
                           peer_sub_keys1.astype(BF16), peer_sub_keys2.astype(BF16))
    outT = _peer(xnT, peer_down.astype(BF16), peer_up.T.astype(BF16), rk, cn, e1, e2)
    return x2, outT.T


def kernel(x, norm_mix_g, w_in, conv_w, norm_conv_out_g, norm_attn_out_g, w_out, norm_ffn_g,
           peer_w_query, peer_sub_keys1, peer_sub_keys2, peer_down, peer_up, final_norm_g):
    batch, seq, d = x.shape
    depth = w_in.shape[0]
    assert d == D_MODEL and seq % MOBA_BLOCK == 0 and seq % TM_MIX == 0
    assert seq // MOBA_BLOCK > MOBA_TOPK
    x2d = x.reshape(batch * seq, d)
    for l in range(depth):
        x2, peer = _layer(x2d, batch, seq, norm_mix_g[l], w_in[l], conv_w[l], norm_conv_out_g[l],
                          norm_attn_out_g[l], w_out[l], norm_ffn_g[l], peer_w_query[l],
                          peer_sub_keys1[l], peer_sub_keys2[l], peer_down[l], peer_up[l])
        if l + 1 < depth:
            x2d = x2 + peer
    out = _final(x2, peer, final_norm_g.reshape(1, D_MODEL))
    return out.reshape(batch, seq, d)
```

```python
import functools

import numpy as np
import jax
import jax.numpy as jnp
from jax import lax
from jax.experimental import pallas as pl
from jax.experimental.pallas import tpu as pltpu

F32 = jnp.float32
BF16 = jnp.bfloat16

D_MODEL = 1024
D_CONV = 512
D_ATTN = 512
D_PROJ = 3 * D_CONV + 3 * D_ATTN
CONV_WIDTH = 3
ATTN_HEADS = 8
HEAD_DIM = 64
MOBA_BLOCK = 256
MOBA_TOPK = 3
PEER_HEADS = 8
PEER_NKEYS = 128
PEER_N = PEER_NKEYS * PEER_NKEYS
PEER_DKEY = 256
PEER_HALF = PEER_DKEY // 2
PEER_TOPK = 16
EPS = 1e-6
NEG = -1e30

VMEM_LIMIT_BYTES = 48 * 1024 * 1024
PEER_VMEM_LIMIT_BYTES = 56 * 1024 * 1024
SUBLANES = 8

TM_INPROJ = 512
TM_MIX = 512
TM_PREP = 256
TM_PEER = 512
PEER_ROWS_PER_CHUNK = 4
PEER_CHUNK = PEER_ROWS_PER_CHUNK * PEER_NKEYS
PEER_CHUNKS_PER_STEP = 4
PEER_EB = PEER_CHUNKS_PER_STEP * PEER_CHUNK
BF16_ROWS = 16
PEER_LANE_TILE = 256
TM_FINAL = 512

_CAND_PAIRS = [(r1, r2) for r1 in range(PEER_TOPK) for r2 in range(PEER_TOPK)
               if (r1 + 1) * (r2 + 1) <= PEER_TOPK]
_CAND_ROWS = -(-len(_CAND_PAIRS) // SUBLANES) * SUBLANES


def _params(*sem, vmem_limit_bytes=VMEM_LIMIT_BYTES):
    return pltpu.CompilerParams(dimension_semantics=sem, vmem_limit_bytes=vmem_limit_bytes)


def _rms(x, g):
    return x * lax.rsqrt(jnp.mean(x * x, axis=-1, keepdims=True) + EPS) * g


def _inproj_body(x_ref, g_ref, w_ref, b_ref, u_ref, q_ref, k_ref, v_ref):
    h = _rms(x_ref[...], g_ref[...]).astype(BF16)
    proj = jnp.dot(h, w_ref[...], preferred_element_type=F32)
    b_ref[...] = proj[:, 0:D_CONV]
    u_ref[...] = proj[:, D_CONV:2 * D_CONV] * proj[:, 2 * D_CONV:3 * D_CONV]
    o = 3 * D_CONV
    q_ref[...] = proj[:, o:o + D_ATTN].astype(BF16)
    k_ref[...] = proj[:, o + D_ATTN:o + 2 * D_ATTN].astype(BF16)
    v_ref[...] = proj[:, o + 2 * D_ATTN:o + 3 * D_ATTN].astype(BF16)


def _inproj(x2d, g, w_in_b):
    t = x2d.shape[0]
    tm = TM_INPROJ
    row = lambda i: (i, 0)
    fixed = lambda i: (0, 0)
    return pl.pallas_call(
        _inproj_body,
        grid=(t // tm,),
        in_specs=[pl.BlockSpec((tm, D_MODEL), row),
                  pl.BlockSpec((1, D_MODEL), fixed),
                  pl.BlockSpec((D_MODEL, D_PROJ), fixed)],
        out_specs=[pl.BlockSpec((tm, D_CONV), row)] * 2 + [pl.BlockSpec((tm, D_ATTN), row)] * 3,
        out_shape=[jax.ShapeDtypeStruct((t, D_CONV), F32)] * 2
                  + [jax.ShapeDtypeStruct((t, D_ATTN), BF16)] * 3,
        compiler_params=_params("parallel"),
        name="inproj",
    )(x2d, g, w_in_b)


def _moba_body(qT_ref, k_ref, vT_ref, o_ref, *, seq):
    nb = seq // MOBA_BLOCK
    bs = MOBA_BLOCK
    scale = float(1.0 / np.sqrt(HEAD_DIM))
    qT = qT_ref[...]
    blk = lax.broadcasted_iota(jnp.int32, (nb, seq), 0)
    own = lax.shift_right_logical(lax.broadcasted_iota(jnp.int32, (nb, seq), 1),
                                  int(np.log2(bs)))
    avg = jnp.where(own == blk, 1.0 / bs, 0.0).astype(BF16)
    kmean = jnp.dot(avg, k_ref[...], preferred_element_type=F32)
    gate = jnp.dot(kmean.astype(BF16), qT, preferred_element_type=F32)
    past = blk < own
    g1 = jnp.where(past, gate, NEG)
    thr = g1
    for _ in range(MOBA_TOPK - 1):
        m = jnp.max(thr, axis=0, keepdims=True)
        thr = jnp.where(thr == m, NEG, thr)
    thr = jnp.max(thr, axis=0, keepdims=True)
    sel = jnp.where(jnp.logical_and(past, g1 >= thr), 1.0, 0.0)

    kpos = lax.broadcasted_iota(jnp.int32, (bs, bs), 0)
    qpos = lax.broadcasted_iota(jnp.int32, (bs, bs), 1)
    causal = kpos <= qpos
    for i in range(nb):
        n = (i + 1) * bs
        lanes = slice(i * bs, (i + 1) * bs)
        sT = jnp.dot(k_ref[0:n, :], qT[:, lanes], preferred_element_type=F32) * scale
        pieces = []
        for j in range(i):
            keep = sel[j:j + 1, lanes] > 0.5
            pieces.append(jnp.where(keep, sT[j * bs:(j + 1) * bs], NEG))
        pieces.append(jnp.where(causal, sT[i * bs:n], NEG))
        s = pieces[0] if i == 0 else jnp.concatenate(pieces, axis=0)
        m = jnp.max(s, axis=0, keepdims=True)
        p = jnp.exp(s - m)
        l = jnp.sum(p, axis=0, keepdims=True)
        oT = jnp.dot(vT_ref[:, 0:n], p.astype(BF16), preferred_element_type=F32)
        o_ref[:, lanes] = oT / l


def _moba(qT, k, vT):
    b, h, hd, s = qT.shape
    return pl.pallas_call(
        functools.partial(_moba_body, seq=s),
        grid=(b, h),
        in_specs=[pl.BlockSpec((None, None, hd, s), lambda i, j: (i, j, 0, 0)),
                  pl.BlockSpec((None, None, s, hd), lambda i, j: (i, j, 0, 0)),
                  pl.BlockSpec((None, None, hd, s), lambda i, j: (i, j, 0, 0))],
        out_specs=pl.BlockSpec((None, None, hd, s), lambda i, j: (i, j, 0, 0)),
        out_shape=jax.ShapeDtypeStruct((b, h, hd, s), F32),
        compiler_params=_params("parallel", "parallel"),
        name="moba",
    )(qT, k, vT)


def _mix_body(x_ref, b_ref, u_ref, uh_ref, ya_ref, cw_ref, gc_ref, ga_ref, wo_ref, gf_ref,
              x2_ref, xn_ref, uext_ref, *, tiles_per_seq):
    tm = u_ref.shape[0]
    first = (pl.program_id(0) % tiles_per_seq) == 0
    uext_ref[0:SUBLANES, :] = jnp.where(first, 0.0, uh_ref[...])
    u = u_ref[...]
    uext_ref[SUBLANES:SUBLANES + tm, :] = u
    cw = cw_ref[...]
    y = (cw[2:3, :] * u
         + cw[1:2, :] * uext_ref[SUBLANES - 1:SUBLANES - 1 + tm, :]
         + cw[0:1, :] * uext_ref[SUBLANES - 2:SUBLANES - 2 + tm, :])
    rc = _rms(b_ref[...] * y, gc_ref[...]).astype(BF16)
    ra = _rms(ya_ref[...], ga_ref[...]).astype(BF16)
    mix = (jnp.dot(rc, wo_ref[0:D_CONV, :], preferred_element_type=F32)
           + jnp.dot(ra, wo_ref[D_CONV:D_CONV + D_ATTN, :], preferred_element_type=F32))
    x2 = x_ref[...] + mix
    x2_ref[...] = x2
    xn_ref[...] = _rms(x2, gf_ref[...]).astype(BF16)


def _mix(x2d, bgate, u, yattn, conv_w, gc, ga, w_out_b, gf, seq):
    t = x2d.shape[0]
    tm = TM_MIX
    row = lambda i: (i, 0)
    fixed = lambda i: (0, 0)
    halo = lambda i: (jnp.maximum(i * (tm // SUBLANES) - 1, 0), 0)
    return pl.pallas_call(
        functools.partial(_mix_body, tiles_per_seq=seq // tm),
        grid=(t // tm,),
        in_specs=[pl.BlockSpec((tm, D_MODEL), row),
                  pl.BlockSpec((tm, D_CONV), row),
                  pl.BlockSpec((tm, D_CONV), row),
                  pl.BlockSpec((SUBLANES, D_CONV), halo),
                  pl.BlockSpec((tm, D_ATTN), row),
                  pl.BlockSpec((CONV_WIDTH, D_CONV), fixed),
                  pl.BlockSpec((1, D_CONV), fixed),
                  pl.BlockSpec((1, D_ATTN), fixed),
                  pl.BlockSpec((D_MODEL, D_MODEL), fixed),
                  pl.BlockSpec((1, D_MODEL), fixed)],
        out_specs=[pl.BlockSpec((tm, D_MODEL), row), pl.BlockSpec((tm, D_MODEL), row)],
        out_shape=[jax.ShapeDtypeStruct((t, D_MODEL), F32),
                   jax.ShapeDtypeStruct((t, D_MODEL), BF16)],
        scratch_shapes=[pltpu.VMEM((tm + SUBLANES, D_CONV), F32)],
        compiler_params=_params("parallel"),
        name="mix",
    )(x2d, bgate, u, u, yattn, conv_w, gc, ga, w_out_b, gf)


def _top_desc(s, n, dst_ref):
    for r in range(n):
        m = jnp.max(s, axis=0, keepdims=True)
        dst_ref[r:r + 1, :] = m
        s = jnp.where(s == m, NEG, s)


def _prep_body(xnT_ref, wqT_ref, k1_ref, k2_ref, rk_ref, cn_ref, e1_ref, e2_ref,
               qry_ref, v1_ref, v2_ref, cand_ref, top_ref):
    qry_ref[...] = jnp.dot(wqT_ref[...], xnT_ref[...], preferred_element_type=F32)
    last = PEER_TOPK - 1
    for h in range(PEER_HEADS):
        o = h * PEER_DKEY
        q1 = qry_ref[o:o + PEER_HALF, :].astype(BF16)
        q2 = qry_ref[o + PEER_HALF:o + PEER_DKEY, :].astype(BF16)
        s1 = jnp.dot(k1_ref[h], q1, preferred_element_type=F32)
        s2 = jnp.dot(k2_ref[h], q2, preferred_element_type=F32)
        _top_desc(s1, PEER_TOPK, v1_ref)
        _top_desc(s2, PEER_TOPK, v2_ref)
        for idx, (r1, r2) in enumerate(_CAND_PAIRS):
            cand_ref[idx:idx + 1, :] = v1_ref[r1:r1 + 1, :] + v2_ref[r2:r2 + 1, :]
        pad = _CAND_ROWS - len(_CAND_PAIRS)
        if pad:
            cand_ref[len(_CAND_PAIRS):_CAND_ROWS, :] = jnp.full((pad, cand_ref.shape[1]), NEG, F32)
        _top_desc(cand_ref[...], PEER_TOPK, top_ref)
        tops = top_ref[...]
        tau = tops[last:last + 1, :]
        z = jnp.sum(jnp.exp(tops - tops[0:1, :]), axis=0, keepdims=True)
        in1 = s1 >= v1_ref[last:last + 1, :]
        in2 = s2 >= v2_ref[last:last + 1, :]
        cnt = jnp.zeros_like(s1)
        rank2 = jnp.zeros_like(s2)
        for r in range(PEER_TOPK):
            v2r = v2_ref[r:r + 1, :]
            cnt = cnt + jnp.where(s1 + v2r >= tau, 1.0, 0.0)
            rank2 = rank2 + jnp.where(v2r > s2, 1.0, 0.0)
        rk_ref[h] = rank2.astype(BF16)
        cn_ref[h] = jnp.where(in1, cnt, 0.0)
        e1_ref[h] = jnp.where(in1, jnp.exp(s1 - v1_ref[0:1, :]), 0.0) / z
        e2_ref[h] = jnp.where(in2, jnp.exp(s2 - v2_ref[0:1, :]), 0.0).astype(BF16)


def _prep(xnT, wqT_b, k1_b, k2_b):
    t = xnT.shape[1]
    tm = TM_PREP
    col = lambda i: (0, i)
    col3 = lambda i: (0, 0, i)
    hk = (PEER_HEADS, PEER_NKEYS, tm)
    return pl.pallas_call(
        _prep_body,
        grid=(t // tm,),
        in_specs=[pl.BlockSpec((D_MODEL, tm), col),
                  pl.BlockSpec((PEER_HEADS * PEER_DKEY, D_MODEL), lambda i: (0, 0)),
                  pl.BlockSpec((PEER_HEADS, PEER_NKEYS, PEER_HALF), lambda i: (0, 0, 0)),
                  pl.BlockSpec((PEER_HEADS, PEER_NKEYS, PEER_HALF), lambda i: (0, 0, 0))],
        out_specs=[pl.BlockSpec(hk, col3)] * 4,
        out_shape=[jax.ShapeDtypeStruct((PEER_HEADS, PEER_NKEYS, t), dt) for dt in (BF16, F32, F32, BF16)],
        scratch_shapes=[pltpu.VMEM((PEER_HEADS * PEER_DKEY, tm), F32),
                        pltpu.VMEM((PEER_TOPK, tm), F32),
                        pltpu.VMEM((PEER_TOPK, tm), F32),
                        pltpu.VMEM((_CAND_ROWS, tm), F32),
                        pltpu.VMEM((PEER_TOPK, tm), F32)],
        compiler_params=_params("parallel"),
        name="prep",
    )(xnT, wqT_b, k1_b, k2_b)


def _peer_hidden(c, xnT_ref, dn_ref, h_ref):
    rows = slice(c * PEER_CHUNK, (c + 1) * PEER_CHUNK)
    h_ref[c] = jnp.dot(dn_ref[rows, :], xnT_ref[...], preferred_element_type=F32)


def _peer_gate_chunk(blk, c, h_ref, rk_ref, cn_ref, e1_ref, e2_ref, a_ref):
    tm = a_ref.shape[1]
    sqrt_half = float(np.sqrt(0.5))
    slabs = PEER_NKEYS // BF16_ROWS
    zero = jnp.zeros((), BF16)
    for r in range(PEER_ROWS_PER_CHUNK):
        i1 = (blk * PEER_CHUNKS_PER_STEP + c) * PEER_ROWS_PER_CHUNK + r
        hr = h_ref[c, r * PEER_NKEYS:(r + 1) * PEER_NKEYS, :]
        act = (0.5 * hr * (1.0 + lax.erf(hr * sqrt_half))).astype(BF16)
        for lt in range(tm // PEER_LANE_TILE):
            lanes = slice(lt * PEER_LANE_TILE, (lt + 1) * PEER_LANE_TILE)
            w = [None] * slabs
            for h in range(PEER_HEADS):
                c16 = jnp.broadcast_to(cn_ref[h, pl.ds(i1, 1), lanes],
                                       (BF16_ROWS, PEER_LANE_TILE)).astype(BF16)
                e16 = jnp.broadcast_to(e1_ref[h, pl.ds(i1, 1), lanes],
                                       (BF16_ROWS, PEER_LANE_TILE)).astype(BF16)
                for s in range(slabs):
                    sl = slice(s * BF16_ROWS, (s + 1) * BF16_ROWS)
                    term = jnp.where(rk_ref[h, sl, lanes] < c16, e2_ref[h, sl, lanes] * e16, zero)
                    w[s] = term if w[s] is None else w[s] + term
            base = c * PEER_CHUNK + r * PEER_NKEYS
            for s in range(slabs):
                sl = slice(s * BF16_ROWS, (s + 1) * BF16_ROWS)
                a_ref[base + s * BF16_ROWS:base + (s + 1) * BF16_ROWS, lanes] = w[s] * act[sl, lanes]


def _peer_up_piece(p, upT_ref, a_ref, o_ref):
    half = o_ref.shape[0] // 2
    rows = slice((p // 2) * half, (p // 2 + 1) * half)
    lanes = slice((p % 2) * PEER_LANE_TILE, (p % 2 + 1) * PEER_LANE_TILE)
    o_ref[rows, lanes] += jnp.dot(upT_ref[rows, :], a_ref[:, lanes], preferred_element_type=F32)


def _peer_step(blk, xnT_ref, dn_ref, upT_ref, rk_ref, cn_ref, e1_ref, e2_ref, o_ref, h_ref,
               a_out, a_in):
    nch = PEER_CHUNKS_PER_STEP
    assert nch == 4 and o_ref.shape[1] == 2 * PEER_LANE_TILE
    if a_out is not None:
        _peer_hidden(0, xnT_ref, dn_ref, h_ref)
    for c in range(nch):
        if a_out is not None:
            if c + 1 < nch:
                _peer_hidden(c + 1, xnT_ref, dn_ref, h_ref)
            _peer_gate_chunk(blk, c, h_ref, rk_ref, cn_ref, e1_ref, e2_ref, a_out)
        if a_in is not None:
            _peer_up_piece(c, upT_ref, a_in, o_ref)


def _peer_body(xnT_ref, dn_ref, upT_ref, rk_ref, cn_ref, e1_ref, e2_ref, o_ref, a0_ref, a1_ref,
               h_ref):
    j = pl.program_id(1)
    nblk = PEER_N // PEER_EB
    step = functools.partial(_peer_step, j, xnT_ref, dn_ref, upT_ref, rk_ref, cn_ref, e1_ref, e2_ref,
                             o_ref, h_ref)
    even = lax.rem(j, 2) == 0
    inner = jnp.logical_and(j > 0, j < nblk)

    @pl.when(j == 0)
    def _():
        o_ref[...] = jnp.zeros_like(o_ref)
        step(a0_ref, None)

    @pl.when(jnp.logical_and(inner, even))
    def _():
        step(a0_ref, a1_ref)

    @pl.when(jnp.logical_and(inner, jnp.logical_not(even)))
    def _():
        step(a1_ref, a0_ref)

    @pl.when(j == nblk)
    def _():
        step(None, a1_ref if (nblk - 1) % 2 else a0_ref)


def _peer(xnT, down_b, upT_b, rk, cn, e1, e2):
    t = xnT.shape[1]
    tm = TM_PEER
    nblk = PEER_N // PEER_EB
    hk = (PEER_HEADS, PEER_NKEYS, tm)
    tok3 = lambda i, j: (0, 0, i)
    return pl.pallas_call(
        _peer_body,
        grid=(t // tm, nblk + 1),
        in_specs=[pl.BlockSpec((D_MODEL, tm), lambda i, j: (0, i)),
                  pl.BlockSpec((PEER_EB, D_MODEL), lambda i, j: (jnp.minimum(j, nblk - 1), 0)),
                  pl.BlockSpec((D_MODEL, PEER_EB), lambda i, j: (0, jnp.maximum(j - 1, 0))),
                  pl.BlockSpec(hk, tok3), pl.BlockSpec(hk, tok3),
                  pl.BlockSpec(hk, tok3), pl.BlockSpec(hk, tok3)],
        out_specs=pl.BlockSpec((D_MODEL, tm), lambda i, j: (0, i)),
        out_shape=jax.ShapeDtypeStruct((D_MODEL, t), F32),
        scratch_shapes=[pltpu.VMEM((PEER_EB, tm), BF16), pltpu.VMEM((PEER_EB, tm), BF16),
                        pltpu.VMEM((PEER_CHUNKS_PER_STEP, PEER_CHUNK, tm), F32)],
        compiler_params=_params("parallel", "arbitrary", vmem_limit_bytes=PEER_VMEM_LIMIT_BYTES),
        name="peer",
    )(xnT, down_b, upT_b, rk, cn, e1, e2)


def _final_body(x2_ref, p_ref, g_ref, o_ref):
    o_ref[...] = _rms(x2_ref[...] + p_ref[...], g_ref[...])


def _final(x2, peer, g):
    t = x2.shape[0]
    tm = TM_FINAL
    row = lambda i: (i, 0)
    return pl.pallas_call(
        _final_body,
        grid=(t // tm,),
        in_specs=[pl.BlockSpec((tm, D_MODEL), row), pl.BlockSpec((tm, D_MODEL), row),
                  pl.BlockSpec((1, D_MODEL), lambda i: (0, 0))],
        out_specs=pl.BlockSpec((tm, D_MODEL), row),
        out_shape=jax.ShapeDtypeStruct((t, D_MODEL), F32),
        compiler_params=_params("parallel"),
        name="final",
    )(x2, peer, g)


def _layer(x2d, batch, seq, norm_mix_g, w_in, conv_w, norm_conv_out_g, norm_attn_out_g, w_out,
           norm_ffn_g, peer_w_query, peer_sub_keys1, peer_sub_keys2, peer_down, peer_up):
    t = batch * seq
    bgate, u, q, k, v = _inproj(x2d, norm_mix_g.reshape(1, D_MODEL), w_in.astype(BF16))
    heads = lambda a: a.reshape(batch, seq, ATTN_HEADS, HEAD_DIM)
    qT = heads(q).transpose(0, 2, 3, 1)
    kh = heads(k).transpose(0, 2, 1, 3)
    vT = heads(v).transpose(0, 2, 3, 1)
    yT = _moba(qT, kh, vT)
    yattn = yT.transpose(0, 3, 1, 2).reshape(t, D_ATTN)
    x2, xn = _mix(x2d, bgate, u, yattn, conv_w, norm_conv_out_g.reshape(1, D_CONV),
                  norm_attn_out_g.reshape(1, D_ATTN), w_out.astype(BF16),
                  norm_ffn_g.reshape(1, D_MODEL), seq)
    xnT = xn.T
    rk, cn, e1, e2 = _prep(xnT, peer_w_query.T.astype(BF16),
                           peer_sub_keys1.astype(BF16), peer_sub_keys2.astype(BF16))
    outT = _peer(xnT, peer_down.astype(BF16), peer_up.T.astype(BF16), rk, cn, e1, e2)
    return x2, outT.T


def kernel(x, norm_mix_g, w_in, conv_w, norm_conv_out_g, norm_attn_out_g, w_out, norm_ffn_g,
           peer_w_query, peer_sub_keys1, peer_sub_keys2, peer_down, peer_up, final_norm_g):
    batch, seq, d = x.shape
    depth = w_in.shape[0]
    assert d == D_MODEL and seq % MOBA_BLOCK == 0 and seq % TM_MIX == 0
    assert seq // MOBA_BLOCK > MOBA_TOPK
    x2d = x.reshape(batch * seq, d)
    for l in range(depth):
        x2, peer = _layer(x2d, batch, seq, norm_mix_g[l], w_in[l], conv_w[l], norm_conv_out_g[l],
                          norm_attn_out_g[l], w_out[l], norm_ffn_g[l], peer_w_query[l],
                          peer_sub_keys1[l], peer_sub_keys2[l], peer_down[l], peer_up[l])
        if l + 1 < depth:
            x2d = x2 + peer
    out = _final(x2, peer, final_norm_g.reshape(1, D_MODEL))
    return out.reshape(batch, seq, d)
```

```python
import functools

import numpy as np
import jax
import jax.numpy as jnp
from jax import lax
from jax.experimental import pallas as pl
from jax.experimental.pallas import tpu as pltpu

F32 = jnp.float32
BF16 = jnp.bfloat16

D_MODEL = 1024
D_CONV = 512
D_ATTN = 512
D_PROJ = 3 * D_CONV + 3 * D_ATTN
CONV_WIDTH = 3
ATTN_HEADS = 8
HEAD_DIM = 64
MOBA_BLOCK = 256
MOBA_TOPK = 3
MOBA_HEADS_PER_STEP = 2
PEER_HEADS = 8
PEER_NKEYS = 128
PEER_N = PEER_NKEYS * PEER_NKEYS
PEER_DKEY = 256
PEER_HALF = PEER_DKEY // 2
PEER_TOPK = 16
EPS = 1e-6
NEG = -1e30
GELU_HALF = 0.5

VMEM_LIMIT_BYTES = 48 * 1024 * 1024
PEER_VMEM_LIMIT_BYTES = 56 * 1024 * 1024
SUBLANES = 8

TM_INPROJ = 512
TM_MIX = 512
TM_PREP = 256
TM_PEER = 512
PEER_ROWS_PER_CHUNK = 4
PEER_CHUNK = PEER_ROWS_PER_CHUNK * PEER_NKEYS
PEER_CHUNKS_PER_STEP = 4
PEER_EB = PEER_CHUNKS_PER_STEP * PEER_CHUNK
BF16_ROWS = 16
PEER_LANE_TILE = 256
PEER_GATE_LANES = 256
PEER_SLAB_LANES = 128
TM_FINAL = 512

_CAND_PAIRS = [(r1, r2) for r1 in range(PEER_TOPK) for r2 in range(PEER_TOPK)
               if (r1 + 1) * (r2 + 1) <= PEER_TOPK]
_CAND_ROWS = -(-len(_CAND_PAIRS) // SUBLANES) * SUBLANES


def _params(*sem, vmem_limit_bytes=VMEM_LIMIT_BYTES):
    return pltpu.CompilerParams(dimension_semantics=sem, vmem_limit_bytes=vmem_limit_bytes)


def _rms(x, g):
    return x * lax.rsqrt(jnp.mean(x * x, axis=-1, keepdims=True) + EPS) * g


def _inproj_body(x_ref, g_ref, w_ref, b_ref, u_ref, q_ref, k_ref, v_ref):
    h = _rms(x_ref[...], g_ref[...]).astype(BF16)
    proj = jnp.dot(h, w_ref[...], preferred_element_type=F32)
    b_ref[...] = proj[:, 0:D_CONV]
    u_ref[...] = proj[:, D_CONV:2 * D_CONV] * proj[:, 2 * D_CONV:3 * D_CONV]
    o = 3 * D_CONV
    q_ref[...] = proj[:, o:o + D_ATTN].T.astype(BF16)
    k_ref[...] = proj[:, o + D_ATTN:o + 2 * D_ATTN].astype(BF16)
    v_ref[...] = proj[:, o + 2 * D_ATTN:o + 3 * D_ATTN].T.astype(BF16)


def _inproj(x2d, g, w_in_b, batch, seq):
    t = x2d.shape[0]
    tm = TM_INPROJ
    tps = seq // tm
    row = lambda i: (i, 0)
    fixed = lambda i: (0, 0)
    chan = lambda i: (i // tps, 0, i % tps)
    return pl.pallas_call(
        _inproj_body,
        grid=(t // tm,),
        in_specs=[pl.BlockSpec((tm, D_MODEL), row),
                  pl.BlockSpec((1, D_MODEL), fixed),
                  pl.BlockSpec((D_MODEL, D_PROJ), fixed)],
        out_specs=[pl.BlockSpec((tm, D_CONV), row), pl.BlockSpec((tm, D_CONV), row),
                   pl.BlockSpec((None, D_ATTN, tm), chan), pl.BlockSpec((tm, D_ATTN), row),
                   pl.BlockSpec((None, D_ATTN, tm), chan)],
        out_shape=[jax.ShapeDtypeStruct((t, D_CONV), F32), jax.ShapeDtypeStruct((t, D_CONV), F32),
                   jax.ShapeDtypeStruct((batch, D_ATTN, seq), BF16),
                   jax.ShapeDtypeStruct((t, D_ATTN), BF16),
                   jax.ShapeDtypeStruct((batch, D_ATTN, seq), BF16)],
        compiler_params=_params("parallel"),
        name="inproj",
    )(x2d, g, w_in_b)


def _moba_head(qT, k_ref, vT_ref, o_ref, rows, seq):
    nb = seq // MOBA_BLOCK
    bs = MOBA_BLOCK
    scale = float(1.0 / np.sqrt(HEAD_DIM))
    blk = lax.broadcasted_iota(jnp.int32, (nb, seq), 0)
    own = lax.shift_right_logical(lax.broadcasted_iota(jnp.int32, (nb, seq), 1),
                                  int(np.log2(bs)))
    avg = jnp.where(own == blk, 1.0 / bs, 0.0).astype(BF16)
    kmean = jnp.dot(avg, k_ref[...], preferred_element_type=F32)
    gate = jnp.dot(kmean.astype(BF16), qT, preferred_element_type=F32)
    past = blk < own
    g1 = jnp.where(past, gate, NEG)
    thr = g1
    for _ in range(MOBA_TOPK - 1):
        m = jnp.max(thr, axis=0, keepdims=True)
        thr = jnp.where(thr == m, NEG, thr)
    thr = jnp.max(thr, axis=0, keepdims=True)
    sel = jnp.where(jnp.logical_and(past, g1 >= thr), 1.0, 0.0)

    kpos = lax.broadcasted_iota(jnp.int32, (bs, bs), 0)
    qpos = lax.broadcasted_iota(jnp.int32, (bs, bs), 1)
    causal = kpos <= qpos
    for i in range(nb):
        n = (i + 1) * bs
        lanes = slice(i * bs, (i + 1) * bs)
        sT = jnp.dot(k_ref[0:n, :], qT[:, lanes], preferred_element_type=F32) * scale
        pieces = []
        for j in range(i):
            keep = sel[j:j + 1, lanes] > 0.5
            pieces.append(jnp.where(keep, sT[j * bs:(j + 1) * bs], NEG))
        pieces.append(jnp.where(causal, sT[i * bs:n], NEG))
        s = pieces[0] if i == 0 else jnp.concatenate(pieces, axis=0)
        m = jnp.max(s, axis=0, keepdims=True)
        p = jnp.exp(s - m)
        l = jnp.sum(p, axis=0, keepdims=True)
        oT = jnp.dot(vT_ref[rows, 0:n], p.astype(BF16), preferred_element_type=F32)
        o_ref[rows, lanes] = oT / l


def _moba_body(qT_ref, k_ref, vT_ref, o_ref, *, seq):
    chan = lax.broadcasted_iota(jnp.int32, (MOBA_HEADS_PER_STEP * HEAD_DIM, seq), 0)
    q2 = qT_ref[...]
    for hh in range(MOBA_HEADS_PER_STEP):
        rows = slice(hh * HEAD_DIM, (hh + 1) * HEAD_DIM)
        mine = jnp.logical_and(chan >= rows.start, chan < rows.stop)
        _moba_head(jnp.where(mine, q2, jnp.zeros((), BF16)), k_ref, vT_ref, o_ref, rows, seq)


def _moba(qT, k, vT):
    b, d, s = qT.shape
    w = MOBA_HEADS_PER_STEP * HEAD_DIM
    chan = lambda i, j: (i, j, 0)
    return pl.pallas_call(
        functools.partial(_moba_body, seq=s),
        grid=(b, d // w),
        in_specs=[pl.BlockSpec((None, w, s), chan),
                  pl.BlockSpec((s, w), lambda i, j: (i, j)),
                  pl.BlockSpec((None, w, s), chan)],
        out_specs=pl.BlockSpec((None, w, s), chan),
        out_shape=jax.ShapeDtypeStruct((b, d, s), F32),
        compiler_params=_params("parallel", "parallel"),
        name="moba",
    )(qT, k, vT)


def _mix_body(x_ref, b_ref, u_ref, uh_ref, ya_ref, cw_ref, gc_ref, ga_ref, wo_ref, gf_ref,
              x2_ref, xn_ref, uext_ref, *, tiles_per_seq):
    tm = u_ref.shape[0]
    first = (pl.program_id(0) % tiles_per_seq) == 0
    uext_ref[0:SUBLANES, :] = jnp.where(first, 0.0, uh_ref[...])
    u = u_ref[...]
    uext_ref[SUBLANES:SUBLANES + tm, :] = u
    cw = cw_ref[...]
    y = (cw[2:3, :] * u
         + cw[1:2, :] * uext_ref[SUBLANES - 1:SUBLANES - 1 + tm, :]
         + cw[0:1, :] * uext_ref[SUBLANES - 2:SUBLANES - 2 + tm, :])
    rc = _rms(b_ref[...] * y, gc_ref[...]).astype(BF16)
    ra = _rms(ya_ref[...].T, ga_ref[...]).astype(BF16)
    mix = (jnp.dot(rc, wo_ref[0:D_CONV, :], preferred_element_type=F32)
           + jnp.dot(ra, wo_ref[D_CONV:D_CONV + D_ATTN, :], preferred_element_type=F32))
    x2 = x_ref[...] + mix
    x2_ref[...] = x2
    xn_ref[...] = _rms(x2, gf_ref[...]).T.astype(BF16)


def _mix(x2d, bgate, u, yattnT, conv_w, gc, ga, w_out_b, gf, seq):
    t = x2d.shape[0]
    tm = TM_MIX
    tps = seq // tm
    row = lambda i: (i, 0)
    fixed = lambda i: (0, 0)
    halo = lambda i: (jnp.maximum(i * (tm // SUBLANES) - 1, 0), 0)
    return pl.pallas_call(
        functools.partial(_mix_body, tiles_per_seq=tps),
        grid=(t // tm,),
        in_specs=[pl.BlockSpec((tm, D_MODEL), row),
                  pl.BlockSpec((tm, D_CONV), row),
                  pl.BlockSpec((tm, D_CONV), row),
                  pl.BlockSpec((SUBLANES, D_CONV), halo),
                  pl.BlockSpec((None, D_ATTN, tm), lambda i: (i // tps, 0, i % tps)),
                  pl.BlockSpec((CONV_WIDTH, D_CONV), fixed),
                  pl.BlockSpec((1, D_CONV), fixed),
                  pl.BlockSpec((1, D_ATTN), fixed),
                  pl.BlockSpec((D_MODEL, D_MODEL), fixed),
                  pl.BlockSpec((1, D_MODEL), fixed)],
        out_specs=[pl.BlockSpec((tm, D_MODEL), row), pl.BlockSpec((D_MODEL, tm), lambda i: (0, i))],
        out_shape=[jax.ShapeDtypeStruct((t, D_MODEL), F32),
                   jax.ShapeDtypeStruct((D_MODEL, t), BF16)],
        scratch_shapes=[pltpu.VMEM((tm + SUBLANES, D_CONV), F32)],
        compiler_params=_params("parallel"),
        name="mix",
    )(x2d, bgate, u, u, yattnT, conv_w, gc, ga, w_out_b, gf)


def _top_desc(s, n, dst_ref):
    for r in range(n):
        m = jnp.max(s, axis=0, keepdims=True)
        dst_ref[r:r + 1, :] = m
        s = jnp.where(s == m, NEG, s)


def _prep_body(xnT_ref, wqT_ref, k1_ref, k2_ref, rk_ref, cn_ref, e1_ref, e2_ref,
               qry_ref, v1_ref, v2_ref, cand_ref, top_ref):
    qry_ref[...] = jnp.dot(wqT_ref[...], xnT_ref[...], preferred_element_type=F32)
    last = PEER_TOPK - 1
    for h in range(PEER_HEADS):
        o = h * PEER_DKEY
        q1 = qry_ref[o:o + PEER_HALF, :].astype(BF16)
        q2 = qry_ref[o + PEER_HALF:o + PEER_DKEY, :].astype(BF16)
        s1 = jnp.dot(k1_ref[h], q1, preferred_element_type=F32)
        s2 = jnp.dot(k2_ref[h], q2, preferred_element_type=F32)
        _top_desc(s1, PEER_TOPK, v1_ref)
        _top_desc(s2, PEER_TOPK, v2_ref)
        for idx, (r1, r2) in enumerate(_CAND_PAIRS):
            cand_ref[idx:idx + 1, :] = v1_ref[r1:r1 + 1, :] + v2_ref[r2:r2 + 1, :]
        pad = _CAND_ROWS - len(_CAND_PAIRS)
        if pad:
            cand_ref[len(_CAND_PAIRS):_CAND_ROWS, :] = jnp.full((pad, cand_ref.shape[1]), NEG, F32)
        _top_desc(cand_ref[...], PEER_TOPK, top_ref)
        tops = top_ref[...]
        tau = tops[last:last + 1, :]
        z = jnp.sum(jnp.exp(tops - tops[0:1, :]), axis=0, keepdims=True)
        in1 = s1 >= v1_ref[last:last + 1, :]
        in2 = s2 >= v2_ref[last:last + 1, :]
        cnt = jnp.zeros_like(s1)
        rank2 = jnp.zeros_like(s2)
        for r in range(PEER_TOPK):
            v2r = v2_ref[r:r + 1, :]
            cnt = cnt + jnp.where(s1 + v2r >= tau, 1.0, 0.0)
            rank2 = rank2 + jnp.where(v2r > s2, 1.0, 0.0)
        rk_ref[h] = rank2.astype(BF16)
        cnt = jnp.where(in1, cnt, 0.0)
        e1 = jnp.where(in1, jnp.exp(s1 - v1_ref[0:1, :]), 0.0) * (GELU_HALF / z)
        for lt in range(s1.shape[1] // PEER_SLAB_LANES):
            lanes = slice(lt * PEER_SLAB_LANES, (lt + 1) * PEER_SLAB_LANES)
            cn_ref[h, lt] = cnt[:, lanes]
            e1_ref[h, lt] = e1[:, lanes]
        e2_ref[h] = jnp.where(in2, jnp.exp(s2 - v2_ref[0:1, :]), 0.0).astype(BF16)


def _prep(xnT, wqT_b, k1_b, k2_b):
    t = xnT.shape[1]
    tm = TM_PREP
    col = lambda i: (0, i)
    col3 = lambda i: (0, 0, i)
    slab = lambda i: (0, i, 0, 0)
    hk = (PEER_HEADS, PEER_NKEYS, tm)
    hk_slab = (PEER_HEADS, tm // PEER_SLAB_LANES, PEER_NKEYS, PEER_SLAB_LANES)
    slab_shape = jax.ShapeDtypeStruct((PEER_HEADS, t // PEER_SLAB_LANES, PEER_NKEYS, PEER_SLAB_LANES), F32)
    return pl.pallas_call(
        _prep_body,
        grid=(t // tm,),
        in_specs=[pl.BlockSpec((D_MODEL, tm), col),
                  pl.BlockSpec((PEER_HEADS * PEER_DKEY, D_MODEL), lambda i: (0, 0)),
                  pl.BlockSpec((PEER_HEADS, PEER_NKEYS, PEER_HALF), lambda i: (0, 0, 0)),
                  pl.BlockSpec((PEER_HEADS, PEER_NKEYS, PEER_HALF), lambda i: (0, 0, 0))],
        out_specs=[pl.BlockSpec(hk, col3), pl.BlockSpec(hk_slab, slab), pl.BlockSpec(hk_slab, slab),
                   pl.BlockSpec(hk, col3)],
        out_shape=[jax.ShapeDtypeStruct((PEER_HEADS, PEER_NKEYS, t), BF16), slab_shape, slab_shape,
                   jax.ShapeDtypeStruct((PEER_HEADS, PEER_NKEYS, t), BF16)],
        scratch_shapes=[pltpu.VMEM((PEER_HEADS * PEER_DKEY, tm), F32),
                        pltpu.VMEM((PEER_TOPK, tm), F32),
                        pltpu.VMEM((PEER_TOPK, tm), F32),
                        pltpu.VMEM((_CAND_ROWS, tm), F32),
                        pltpu.VMEM((PEER_TOPK, tm), F32)],
        compiler_params=_params("parallel"),
        name="prep",
    )(xnT, wqT_b, k1_b, k2_b)


def _peer_hidden(c, xnT_ref, dn_ref, h_ref):
    rows = slice(c * PEER_CHUNK, (c + 1) * PEER_CHUNK)
    h_ref[c] = jnp.dot(dn_ref[rows, :], xnT_ref[...], preferred_element_type=F32)


def _row_tile(ref, h, lt, i1):
    per = PEER_GATE_LANES // PEER_SLAB_LANES
    parts = [ref[h, lt * per + k, pl.ds(i1, BF16_ROWS, stride=0), :] for k in range(per)]
    return jnp.concatenate(parts, axis=1).astype(BF16)


def _peer_gate_chunk(blk, c, h_ref, rk_ref, cn_ref, e1_ref, e2_ref, a_ref):
    tm = a_ref.shape[1]
    sqrt_half = float(np.sqrt(0.5))
    slabs = PEER_NKEYS // BF16_ROWS
    zero = jnp.zeros((), BF16)
    for r in range(PEER_ROWS_PER_CHUNK):
        i1 = (blk * PEER_CHUNKS_PER_STEP + c) * PEER_ROWS_PER_CHUNK + r
        hr = h_ref[c, r * PEER_NKEYS:(r + 1) * PEER_NKEYS, :]
        act = (hr * (1.0 + lax.erf(hr * sqrt_half))).astype(BF16)
        for lt in range(tm // PEER_GATE_LANES):
            lanes = slice(lt * PEER_GATE_LANES, (lt + 1) * PEER_GATE_LANES)
            w = [None] * slabs
            for h in range(PEER_HEADS):
                c16 = _row_tile(cn_ref, h, lt, i1)
                e16 = _row_tile(e1_ref, h, lt, i1)
                for s in range(slabs):
                    sl = slice(s * BF16_ROWS, (s + 1) * BF16_ROWS)
                    term = jnp.where(rk_ref[h, sl, lanes] < c16, e2_ref[h, sl, lanes] * e16, zero)
                    w[s] = term if w[s] is None else w[s] + term
            base = c * PEER_CHUNK + r * PEER_NKEYS
            for s in range(slabs):
                sl = slice(s * BF16_ROWS, (s + 1) * BF16_ROWS)
                a_ref[base + s * BF16_ROWS:base + (s + 1) * BF16_ROWS, lanes] = w[s] * act[sl, lanes]


def _peer_up_piece(p, upT_ref, a_ref, o_ref):
    half = o_ref.shape[0] // 2
    rows = slice((p // 2) * half, (p // 2 + 1) * half)
    lanes = slice((p % 2) * PEER_LANE_TILE, (p % 2 + 1) * PEER_LANE_TILE)
    o_ref[rows, lanes] += jnp.dot(upT_ref[rows, :], a_ref[:, lanes], preferred_element_type=F32)


def _peer_step(blk, xnT_ref, dn_ref, upT_ref, rk_ref, cn_ref, e1_ref, e2_ref, o_ref, h_ref,
               a_out, a_in):
    nch = PEER_CHUNKS_PER_STEP
    assert nch == 4 and o_ref.shape[1] == 2 * PEER_LANE_TILE
    if a_out is not None:
        _peer_hidden(0, xnT_ref, dn_ref, h_ref)
    for c in range(nch):
        if a_out is not None:
            if c + 1 < nch:
                _peer_hidden(c + 1, xnT_ref, dn_ref, h_ref)
            _peer_gate_chunk(blk, c, h_ref, rk_ref, cn_ref, e1_ref, e2_ref, a_out)
        if a_in is not None:
            _peer_up_piece(c, upT_ref, a_in, o_ref)


def _peer_body(xnT_ref, dn_ref, upT_ref, rk_ref, cn_ref, e1_ref, e2_ref, o_ref, a0_ref, a1_ref,
               h_ref):
    j = pl.program_id(1)
    nblk = PEER_N // PEER_EB
    step = functools.partial(_peer_step, j, xnT_ref, dn_ref, upT_ref, rk_ref, cn_ref, e1_ref,
                             e2_ref, o_ref, h_ref)
    even = lax.rem(j, 2) == 0
    inner = jnp.logical_and(j > 0, j < nblk)

    @pl.when(j == 0)
    def _():
        o_ref[...] = jnp.zeros_like(o_ref)
        step(a0_ref, None)

    @pl.when(jnp.logical_and(inner, even))
    def _():
        step(a0_ref, a1_ref)

    @pl.when(jnp.logical_and(inner, jnp.logical_not(even)))
    def _():
        step(a1_ref, a0_ref)

    @pl.when(j == nblk)
    def _():
        step(None, a1_ref if (nblk - 1) % 2 else a0_ref)


def _peer(xnT, down_b, upT_b, rk, cn, e1, e2):
    t = xnT.shape[1]
    tm = TM_PEER
    nblk = PEER_N // PEER_EB
    hk = (PEER_HEADS, PEER_NKEYS, tm)
    hk_slab = (PEER_HEADS, tm // PEER_SLAB_LANES, PEER_NKEYS, PEER_SLAB_LANES)
    tok3 = lambda i, j: (0, 0, i)
    slab = lambda i, j: (0, i, 0, 0)
    return pl.pallas_call(
        _peer_body,
        grid=(t // tm, nblk + 1),
        in_specs=[pl.BlockSpec((D_MODEL, tm), lambda i, j: (0, i)),
                  pl.BlockSpec((PEER_EB, D_MODEL), lambda i, j: (jnp.minimum(j, nblk - 1), 0)),
                  pl.BlockSpec((D_MODEL, PEER_EB), lambda i, j: (0, jnp.maximum(j - 1, 0))),
                  pl.BlockSpec(hk, tok3), pl.BlockSpec(hk_slab, slab),
                  pl.BlockSpec(hk_slab, slab), pl.BlockSpec(hk, tok3)],
        out_specs=pl.BlockSpec((D_MODEL, tm), lambda i, j: (0, i)),
        out_shape=jax.ShapeDtypeStruct((D_MODEL, t), F32),
        scratch_shapes=[pltpu.VMEM((PEER_EB, tm), BF16), pltpu.VMEM((PEER_EB, tm), BF16),
                        pltpu.VMEM((PEER_CHUNKS_PER_STEP, PEER_CHUNK, tm), F32)],
        compiler_params=_params("parallel", "arbitrary", vmem_limit_bytes=PEER_VMEM_LIMIT_BYTES),
        name="peer",
    )(xnT, down_b, upT_b, rk, cn, e1, e2)


def _final_body(x2_ref, pT_ref, g_ref, o_ref):
    o_ref[...] = _rms(x2_ref[...] + pT_ref[...].T, g_ref[...])


def _final(x2, peerT, g):
    t = x2.shape[0]
    tm = TM_FINAL
    row = lambda i: (i, 0)
    return pl.pallas_call(
        _final_body,
        grid=(t // tm,),
        in_specs=[pl.BlockSpec((tm, D_MODEL), row), pl.BlockSpec((D_MODEL, tm), lambda i: (0, i)),
                  pl.BlockSpec((1, D_MODEL), lambda i: (0, 0))],
        out_specs=pl.BlockSpec((tm, D_MODEL), row),
        out_shape=jax.ShapeDtypeStruct((t, D_MODEL), F32),
        compiler_params=_params("parallel"),
        name="final",
    )(x2, peerT, g)


def kernel(x, norm_mix_g, w_in, conv_w, norm_conv_out_g, norm_attn_out_g, w_out, norm_ffn_g,
           peer_w_query, peer_sub_keys1, peer_sub_keys2, peer_down, peer_up, final_norm_g):
    batch, seq, d = x.shape
    assert w_in.shape[0] == 1, "one layer: the residual stream between layers is not implemented"
    assert d == D_MODEL and seq % MOBA_BLOCK == 0 and seq % TM_MIX == 0 and seq % TM_INPROJ == 0
    assert seq // MOBA_BLOCK > MOBA_TOPK
    x2d = x.reshape(batch * seq, d)
    bgate, u, qT, k, vT = _inproj(x2d, norm_mix_g[0].reshape(1, D_MODEL), w_in[0].astype(BF16),
                                  batch, seq)
    yT = _moba(qT, k, vT)
    x2, xnT = _mix(x2d, bgate, u, yT, conv_w[0], norm_conv_out_g[0].reshape(1, D_CONV),
                   norm_attn_out_g[0].reshape(1, D_ATTN), w_out[0].astype(BF16),
                   norm_ffn_g[0].reshape(1, D_MODEL), seq)
    rk, cn, e1, e2 = _prep(xnT, peer_w_query[0].T.astype(BF16),
                           peer_sub_keys1[0].astype(BF16), peer_sub_keys2[0].astype(BF16))
    outT = _peer(xnT, peer_down[0].astype(BF16), peer_up[0].T.astype(BF16), rk, cn, e1, e2)
    out = _final(x2, outT, final_norm_g.reshape(1, D_MODEL))
    return out.reshape(batch, seq, d)
```

```python
import functools

import numpy as np
import jax
import jax.numpy as jnp
from jax import lax
from jax.experimental import pallas as pl
from jax.experimental.pallas import tpu as pltpu

F32 = jnp.float32
BF16 = jnp.bfloat16

D_MODEL = 1024
D_CONV = 512
D_ATTN = 512
D_PROJ = 3 * D_CONV + 3 * D_ATTN
CONV_WIDTH = 3
ATTN_HEADS = 8
HEAD_DIM = 64
MOBA_BLOCK = 256
MOBA_TOPK = 3
MOBA_HEADS_PER_STEP = 2
PEER_HEADS = 8
PEER_NKEYS = 128
PEER_N = PEER_NKEYS * PEER_NKEYS
PEER_DKEY = 256
PEER_HALF = PEER_DKEY // 2
PEER_TOPK = 16
EPS = 1e-6
NEG = -1e30
GELU_HALF = 0.5

VMEM_LIMIT_BYTES = 48 * 1024 * 1024
PEER_VMEM_LIMIT_BYTES = 56 * 1024 * 1024
SUBLANES = 8

TM_INPROJ = 512
TM_MIX = 512
TM_PREP = 256
TM_PEER = 512
PEER_ROWS_PER_CHUNK = 4
PEER_CHUNK = PEER_ROWS_PER_CHUNK * PEER_NKEYS
PEER_CHUNKS_PER_STEP = 4
PEER_EB = PEER_CHUNKS_PER_STEP * PEER_CHUNK
BF16_ROWS = 16
PEER_LANE_TILE = 256
PEER_GATE_LANES = 256
PEER_SLAB_LANES = 128
TM_FINAL = 512

_CAND_PAIRS = [(r1, r2) for r1 in range(PEER_TOPK) for r2 in range(PEER_TOPK)
               if (r1 + 1) * (r2 + 1) <= PEER_TOPK]


def _params(*sem, vmem_limit_bytes=VMEM_LIMIT_BYTES):
    return pltpu.CompilerParams(dimension_semantics=sem, vmem_limit_bytes=vmem_limit_bytes)


def _rms(x, g):
    return x * lax.rsqrt(jnp.mean(x * x, axis=-1, keepdims=True) + EPS) * g


def _inproj_body(x_ref, g_ref, w_ref, b_ref, u_ref, q_ref, k_ref, v_ref):
    h = _rms(x_ref[...], g_ref[...]).astype(BF16)
    proj = jnp.dot(h, w_ref[...], preferred_element_type=F32)
    b_ref[...] = proj[:, 0:D_CONV]
    u_ref[...] = proj[:, D_CONV:2 * D_CONV] * proj[:, 2 * D_CONV:3 * D_CONV]
    o = 3 * D_CONV
    q_ref[...] = proj[:, o:o + D_ATTN].T.astype(BF16)
    k_ref[...] = proj[:, o + D_ATTN:o + 2 * D_ATTN].astype(BF16)
    v_ref[...] = proj[:, o + 2 * D_ATTN:o + 3 * D_ATTN].T.astype(BF16)


def _inproj(x2d, g, w_in_b, batch, seq):
    t = x2d.shape[0]
    tm = TM_INPROJ
    tps = seq // tm
    row = lambda i: (i, 0)
    fixed = lambda i: (0, 0)
    chan = lambda i: (i // tps, 0, i % tps)
    return pl.pallas_call(
        _inproj_body,
        grid=(t // tm,),
        in_specs=[pl.BlockSpec((tm, D_MODEL), row),
                  pl.BlockSpec((1, D_MODEL), fixed),
                  pl.BlockSpec((D_MODEL, D_PROJ), fixed)],
        out_specs=[pl.BlockSpec((tm, D_CONV), row), pl.BlockSpec((tm, D_CONV), row),
                   pl.BlockSpec((None, D_ATTN, tm), chan), pl.BlockSpec((tm, D_ATTN), row),
                   pl.BlockSpec((None, D_ATTN, tm), chan)],
        out_shape=[jax.ShapeDtypeStruct((t, D_CONV), F32), jax.ShapeDtypeStruct((t, D_CONV), F32),
                   jax.ShapeDtypeStruct((batch, D_ATTN, seq), BF16),
                   jax.ShapeDtypeStruct((t, D_ATTN), BF16),
                   jax.ShapeDtypeStruct((batch, D_ATTN, seq), BF16)],
        compiler_params=_params("parallel"),
        name="inproj",
    )(x2d, g, w_in_b)


def _moba_head(qT, k_ref, vT_ref, o_ref, rows, seq):
    nb = seq // MOBA_BLOCK
    bs = MOBA_BLOCK
    scale = float(1.0 / np.sqrt(HEAD_DIM))
    blk = lax.broadcasted_iota(jnp.int32, (nb, seq), 0)
    own = lax.shift_right_logical(lax.broadcasted_iota(jnp.int32, (nb, seq), 1),
                                  int(np.log2(bs)))
    avg = jnp.where(own == blk, 1.0 / bs, 0.0).astype(BF16)
    kmean = jnp.dot(avg, k_ref[...], preferred_element_type=F32)
    gate = jnp.dot(kmean.astype(BF16), qT, preferred_element_type=F32)
    past = blk < own
    g1 = jnp.where(past, gate, NEG)
    thr = g1
    for _ in range(MOBA_TOPK - 1):
        m = jnp.max(thr, axis=0, keepdims=True)
        thr = jnp.where(thr == m, NEG, thr)
    thr = jnp.max(thr, axis=0, keepdims=True)
    sel = jnp.where(jnp.logical_and(past, g1 >= thr), 1.0, 0.0)

    kpos = lax.broadcasted_iota(jnp.int32, (bs, bs), 0)
    qpos = lax.broadcasted_iota(jnp.int32, (bs, bs), 1)
    causal = kpos <= qpos
    for i in range(nb):
        n = (i + 1) * bs
        lanes = slice(i * bs, (i + 1) * bs)
        sT = jnp.dot(k_ref[0:n, :], qT[:, lanes], preferred_element_type=F32) * scale
        pieces = []
        for j in range(i):
            keep = sel[j:j + 1, lanes] > 0.5
            pieces.append(jnp.where(keep, sT[j * bs:(j + 1) * bs], NEG))
        pieces.append(jnp.where(causal, sT[i * bs:n], NEG))
        s = pieces[0] if i == 0 else jnp.concatenate(pieces, axis=0)
        m = jnp.max(s, axis=0, keepdims=True)
        p = jnp.exp(s - m)
        l = jnp.sum(p, axis=0, keepdims=True)
        oT = jnp.dot(vT_ref[rows, 0:n], p.astype(BF16), preferred_element_type=F32)
        o_ref[rows, lanes] = oT / l


def _moba_body(qT_ref, k_ref, vT_ref, o_ref, *, seq):
    chan = lax.broadcasted_iota(jnp.int32, (MOBA_HEADS_PER_STEP * HEAD_DIM, seq), 0)
    q2 = qT_ref[...]
    for hh in range(MOBA_HEADS_PER_STEP):
        rows = slice(hh * HEAD_DIM, (hh + 1) * HEAD_DIM)
        mine = jnp.logical_and(chan >= rows.start, chan < rows.stop)
        _moba_head(jnp.where(mine, q2, jnp.zeros((), BF16)), k_ref, vT_ref, o_ref, rows, seq)


def _moba(qT, k, vT):
    b, d, s = qT.shape
    w = MOBA_HEADS_PER_STEP * HEAD_DIM
    chan = lambda i, j: (i, j, 0)
    return pl.pallas_call(
        functools.partial(_moba_body, seq=s),
        grid=(b, d // w),
        in_specs=[pl.BlockSpec((None, w, s), chan),
                  pl.BlockSpec((s, w), lambda i, j: (i, j)),
                  pl.BlockSpec((None, w, s), chan)],
        out_specs=pl.BlockSpec((None, w, s), chan),
        out_shape=jax.ShapeDtypeStruct((b, d, s), F32),
        compiler_params=_params("parallel", "parallel"),
        name="moba",
    )(qT, k, vT)


def _mix_body(x_ref, b_ref, u_ref, uh_ref, ya_ref, cw_ref, gc_ref, ga_ref, wo_ref, gf_ref,
              x2_ref, xn_ref, uext_ref, *, tiles_per_seq):
    tm = u_ref.shape[0]
    first = (pl.program_id(0) % tiles_per_seq) == 0
    uext_ref[0:SUBLANES, :] = jnp.where(first, 0.0, uh_ref[...])
    u = u_ref[...]
    uext_ref[SUBLANES:SUBLANES + tm, :] = u
    cw = cw_ref[...]
    y = (cw[2:3, :] * u
         + cw[1:2, :] * uext_ref[SUBLANES - 1:SUBLANES - 1 + tm, :]
         + cw[0:1, :] * uext_ref[SUBLANES - 2:SUBLANES - 2 + tm, :])
    rc = _rms(b_ref[...] * y, gc_ref[...]).astype(BF16)
    ra = _rms(ya_ref[...].T, ga_ref[...]).astype(BF16)
    mix = (jnp.dot(rc, wo_ref[0:D_CONV, :], preferred_element_type=F32)
           + jnp.dot(ra, wo_ref[D_CONV:D_CONV + D_ATTN, :], preferred_element_type=F32))
    x2 = x_ref[...] + mix
    x2_ref[...] = x2
    xn_ref[...] = _rms(x2, gf_ref[...]).T.astype(BF16)


def _mix(x2d, bgate, u, yattnT, conv_w, gc, ga, w_out_b, gf, seq):
    t = x2d.shape[0]
    tm = TM_MIX
    tps = seq // tm
    row = lambda i: (i, 0)
    fixed = lambda i: (0, 0)
    halo = lambda i: (jnp.maximum(i * (tm // SUBLANES) - 1, 0), 0)
    return pl.pallas_call(
        functools.partial(_mix_body, tiles_per_seq=tps),
        grid=(t // tm,),
        in_specs=[pl.BlockSpec((tm, D_MODEL), row),
                  pl.BlockSpec((tm, D_CONV), row),
                  pl.BlockSpec((tm, D_CONV), row),
                  pl.BlockSpec((SUBLANES, D_CONV), halo),
                  pl.BlockSpec((None, D_ATTN, tm), lambda i: (i // tps, 0, i % tps)),
                  pl.BlockSpec((CONV_WIDTH, D_CONV), fixed),
                  pl.BlockSpec((1, D_CONV), fixed),
                  pl.BlockSpec((1, D_ATTN), fixed),
                  pl.BlockSpec((D_MODEL, D_MODEL), fixed),
                  pl.BlockSpec((1, D_MODEL), fixed)],
        out_specs=[pl.BlockSpec((tm, D_MODEL), row), pl.BlockSpec((D_MODEL, tm), lambda i: (0, i))],
        out_shape=[jax.ShapeDtypeStruct((t, D_MODEL), F32),
                   jax.ShapeDtypeStruct((D_MODEL, t), BF16)],
        scratch_shapes=[pltpu.VMEM((tm + SUBLANES, D_CONV), F32)],
        compiler_params=_params("parallel"),
        name="mix",
    )(x2d, bgate, u, u, yattnT, conv_w, gc, ga, w_out_b, gf)


def _oddeven_merge_sort_network(n):
    pairs = []
    p = 1
    while p < n:
        k = p
        while k >= 1:
            for j in range(k % p, n - k, 2 * k):
                for i in range(min(k, n - j - k)):
                    if (i + j) // (2 * p) == (i + j + k) // (2 * p):
                        pairs.append((i + j, i + j + k))
            k //= 2
        p *= 2
    return pairs


_SORT_NET = _oddeven_merge_sort_network(PEER_TOPK)


def _exchange(g, i, j):
    g[i], g[j] = jnp.maximum(g[i], g[j]), jnp.minimum(g[i], g[j])


def _slabs(s):
    return [s[SUBLANES * i:SUBLANES * (i + 1), :] for i in range(s.shape[0] // SUBLANES)]


def _sorted_top(slabs):
    n = PEER_TOPK
    g = list(slabs)
    assert len(g) == n
    for i, j in _SORT_NET:
        _exchange(g, i, j)
    shift = SUBLANES // 2
    while shift >= 1:
        rolled = [pltpu.roll(x, shift, axis=0) for x in g]
        g = [jnp.maximum(g[i], rolled[n - 1 - i]) for i in range(n)]
        d = n // 2
        while d >= 1:
            for i in range(n):
                if i & d == 0:
                    _exchange(g, i, i + d)
            d //= 2
        shift //= 2
    return g


def _kth_largest_packed(cands, k):
    sub = lax.broadcasted_iota(jnp.int32, cands[0].shape, 0)
    packed = []
    for q in range(0, len(cands), SUBLANES):
        group = cands[q:q + SUBLANES]
        x = group[0] if len(group) == SUBLANES else jnp.full_like(cands[0], NEG)
        for p, cnd in enumerate(group):
            if p or len(group) < SUBLANES:
                x = jnp.where(sub == p, cnd, x)
        packed.append(x)
    out = []
    for _ in range(k):
        m = functools.reduce(jnp.maximum, packed)
        m = jnp.max(m, axis=0, keepdims=True)
        out.append(jnp.broadcast_to(m, cands[0].shape))
        packed = [jnp.where(x == m, NEG, x) for x in packed]
    return out


def _prep_head(s1, s2):
    last = PEER_TOPK - 1
    g1, g2 = _slabs(s1), _slabs(s2)
    v1, v2 = _sorted_top(g1), _sorted_top(g2)
    cands = [v1[r1] + v2[r2] for r1, r2 in _CAND_PAIRS]
    tops = _kth_largest_packed(cands, PEER_TOPK)
    tau = tops[last]
    z = functools.reduce(jnp.add, [jnp.exp(t - tops[0]) for t in tops])
    cs = [jnp.zeros_like(tau) for _ in range(PEER_TOPK)]
    for cnd, (r1, _) in zip(cands, _CAND_PAIRS):
        cs[r1] = cs[r1] + jnp.where(cnd >= tau, 1.0, 0.0)
    scale = GELU_HALF / z
    rank2, cnt, e1, e2 = [], [], [], []
    for a, b in zip(g1, g2):
        c = jnp.zeros_like(a)
        for r in range(PEER_TOPK):
            c = jnp.where(a == v1[r], cs[r], c)
        cnt.append(c)
        rk = jnp.full_like(b, float(PEER_TOPK))
        for r in range(last, -1, -1):
            rk = jnp.where(b >= v2[r], float(r), rk)
        rank2.append(rk)
        e1.append(jnp.where(a >= v1[last], jnp.exp(a - v1[0]), 0.0) * scale)
        e2.append(jnp.where(b >= v2[last], jnp.exp(b - v2[0]), 0.0))
    cat = lambda parts: jnp.concatenate(parts, axis=0)
    return cat(rank2), cat(cnt), cat(e1), cat(e2)


def _prep_body(xnT_ref, wqT_ref, k1_ref, k2_ref, rk_ref, cn_ref, e1_ref, e2_ref, qry_ref):
    qry_ref[...] = jnp.dot(wqT_ref[...], xnT_ref[...], preferred_element_type=F32)
    for h in range(PEER_HEADS):
        o = h * PEER_DKEY
        q1 = qry_ref[o:o + PEER_HALF, :].astype(BF16)
        q2 = qry_ref[o + PEER_HALF:o + PEER_DKEY, :].astype(BF16)
        s1 = jnp.dot(k1_ref[h], q1, preferred_element_type=F32)
        s2 = jnp.dot(k2_ref[h], q2, preferred_element_type=F32)
        rank2, cnt, e1, e2 = _prep_head(s1, s2)
        rk_ref[h] = rank2.astype(BF16)
        e2_ref[h] = e2.astype(BF16)
        for lt in range(s1.shape[1] // PEER_SLAB_LANES):
            lanes = slice(lt * PEER_SLAB_LANES, (lt + 1) * PEER_SLAB_LANES)
            cn_ref[h, lt] = cnt[:, lanes]
            e1_ref[h, lt] = e1[:, lanes]


def _prep(xnT, wqT_b, k1_b, k2_b):
    t = xnT.shape[1]
    tm = TM_PREP
    col = lambda i: (0, i)
    col3 = lambda i: (0, 0, i)
    slab = lambda i: (0, i, 0, 0)
    hk = (PEER_HEADS, PEER_NKEYS, tm)
    hk_slab = (PEER_HEADS, tm // PEER_SLAB_LANES, PEER_NKEYS, PEER_SLAB_LANES)
    slab_shape = jax.ShapeDtypeStruct((PEER_HEADS, t // PEER_SLAB_LANES, PEER_NKEYS, PEER_SLAB_LANES), F32)
    return pl.pallas_call(
        _prep_body,
        grid=(t // tm,),
        in_specs=[pl.BlockSpec((D_MODEL, tm), col),
                  pl.BlockSpec((PEER_HEADS * PEER_DKEY, D_MODEL), lambda i: (0, 0)),
                  pl.BlockSpec((PEER_HEADS, PEER_NKEYS, PEER_HALF), lambda i: (0, 0, 0)),
                  pl.BlockSpec((PEER_HEADS, PEER_NKEYS, PEER_HALF), lambda i: (0, 0, 0))],
        out_specs=[pl.BlockSpec(hk, col3), pl.BlockSpec(hk_slab, slab), pl.BlockSpec(hk_slab, slab),
                   pl.BlockSpec(hk, col3)],
        out_shape=[jax.ShapeDtypeStruct((PEER_HEADS, PEER_NKEYS, t), BF16), slab_shape, slab_shape,
                   jax.ShapeDtypeStruct((PEER_HEADS, PEER_NKEYS, t), BF16)],
        scratch_shapes=[pltpu.VMEM((PEER_HEADS * PEER_DKEY, tm), F32)],
        compiler_params=_params("parallel"),
        name="prep",
    )(xnT, wqT_b, k1_b, k2_b)


def _peer_hidden(c, xnT_ref, dn_ref, h_ref):
    rows = slice(c * PEER_CHUNK, (c + 1) * PEER_CHUNK)
    h_ref[c] = jnp.dot(dn_ref[rows, :], xnT_ref[...], preferred_element_type=F32)


def _row_tile(ref, h, lt, i1):
    per = PEER_GATE_LANES // PEER_SLAB_LANES
    parts = [ref[h, lt * per + k, pl.ds(i1, BF16_ROWS, stride=0), :] for k in range(per)]
    return jnp.concatenate(parts, axis=1).astype(BF16)


def _peer_gate_chunk(blk, c, h_ref, rk_ref, cn_ref, e1_ref, e2_ref, a_ref):
    tm = a_ref.shape[1]
    sqrt_half = float(np.sqrt(0.5))
    slabs = PEER_NKEYS // BF16_ROWS
    zero = jnp.zeros((), BF16)
    for r in range(PEER_ROWS_PER_CHUNK):
        i1 = (blk * PEER_CHUNKS_PER_STEP + c) * PEER_ROWS_PER_CHUNK + r
        hr = h_ref[c, r * PEER_NKEYS:(r + 1) * PEER_NKEYS, :]
        act = (hr * (1.0 + lax.erf(hr * sqrt_half))).astype(BF16)
        for lt in range(tm // PEER_GATE_LANES):
            lanes = slice(lt * PEER_GATE_LANES, (lt + 1) * PEER_GATE_LANES)
            w = [None] * slabs
            for h in range(PEER_HEADS):
                c16 = _row_tile(cn_ref, h, lt, i1)
                e16 = _row_tile(e1_ref, h, lt, i1)
                for s in range(slabs):
                    sl = slice(s * BF16_ROWS, (s + 1) * BF16_ROWS)
                    term = jnp.where(rk_ref[h, sl, lanes] < c16, e2_ref[h, sl, lanes] * e16, zero)
                    w[s] = term if w[s] is None else w[s] + term
            base = c * PEER_CHUNK + r * PEER_NKEYS
            for s in range(slabs):
                sl = slice(s * BF16_ROWS, (s + 1) * BF16_ROWS)
                a_ref[base + s * BF16_ROWS:base + (s + 1) * BF16_ROWS, lanes] = w[s] * act[sl, lanes]


def _peer_up_piece(p, upT_ref, a_ref, o_ref):
    half = o_ref.shape[0] // 2
    rows = slice((p // 2) * half, (p // 2 + 1) * half)
    lanes = slice((p % 2) * PEER_LANE_TILE, (p % 2 + 1) * PEER_LANE_TILE)
    o_ref[rows, lanes] += jnp.dot(upT_ref[rows, :], a_ref[:, lanes], preferred_element_type=F32)


def _peer_step(blk, xnT_ref, dn_ref, upT_ref, rk_ref, cn_ref, e1_ref, e2_ref, o_ref, h_ref,
               a_out, a_in):
    nch = PEER_CHUNKS_PER_STEP
    assert nch == 4 and o_ref.shape[1] == 2 * PEER_LANE_TILE
    if a_out is not None:
        _peer_hidden(0, xnT_ref, dn_ref, h_ref)
    for c in range(nch):
        if a_out is not None:
            if c + 1 < nch:
                _peer_hidden(c + 1, xnT_ref, dn_ref, h_ref)
            _peer_gate_chunk(blk, c, h_ref, rk_ref, cn_ref, e1_ref, e2_ref, a_out)
        if a_in is not None:
            _peer_up_piece(c, upT_ref, a_in, o_ref)


def _peer_body(xnT_ref, dn_ref, upT_ref, rk_ref, cn_ref, e1_ref, e2_ref, o_ref, a0_ref, a1_ref,
               h_ref):
    j = pl.program_id(1)
    nblk = PEER_N // PEER_EB
    step = functools.partial(_peer_step, j, xnT_ref, dn_ref, upT_ref, rk_ref, cn_ref, e1_ref,
                             e2_ref, o_ref, h_ref)
    even = lax.rem(j, 2) == 0
    inner = jnp.logical_and(j > 0, j < nblk)

    @pl.when(j == 0)
    def _():
        o_ref[...] = jnp.zeros_like(o_ref)
        step(a0_ref, None)

    @pl.when(jnp.logical_and(inner, even))
    def _():
        step(a0_ref, a1_ref)

    @pl.when(jnp.logical_and(inner, jnp.logical_not(even)))
    def _():
        step(a1_ref, a0_ref)

    @pl.when(j == nblk)
    def _():
        step(None, a1_ref if (nblk - 1) % 2 else a0_ref)


def _peer(xnT, down_b, upT_b, rk, cn, e1, e2):
    t = xnT.shape[1]
    tm = TM_PEER
    nblk = PEER_N // PEER_EB
    hk = (PEER_HEADS, PEER_NKEYS, tm)
    hk_slab = (PEER_HEADS, tm // PEER_SLAB_LANES, PEER_NKEYS, PEER_SLAB_LANES)
    tok3 = lambda i, j: (0, 0, i)
    slab = lambda i, j: (0, i, 0, 0)
    return pl.pallas_call(
        _peer_body,
        grid=(t // tm, nblk + 1),
        in_specs=[pl.BlockSpec((D_MODEL, tm), lambda i, j: (0, i)),
                  pl.BlockSpec((PEER_EB, D_MODEL), lambda i, j: (jnp.minimum(j, nblk - 1), 0)),
                  pl.BlockSpec((D_MODEL, PEER_EB), lambda i, j: (0, jnp.maximum(j - 1, 0))),
                  pl.BlockSpec(hk, tok3), pl.BlockSpec(hk_slab, slab),
                  pl.BlockSpec(hk_slab, slab), pl.BlockSpec(hk, tok3)],
        out_specs=pl.BlockSpec((D_MODEL, tm), lambda i, j: (0, i)),
        out_shape=jax.ShapeDtypeStruct((D_MODEL, t), F32),
        scratch_shapes=[pltpu.VMEM((PEER_EB, tm), BF16), pltpu.VMEM((PEER_EB, tm), BF16),
                        pltpu.VMEM((PEER_CHUNKS_PER_STEP, PEER_CHUNK, tm), F32)],
        compiler_params=_params("parallel", "arbitrary", vmem_limit_bytes=PEER_VMEM_LIMIT_BYTES),
        name="peer",
    )(xnT, down_b, upT_b, rk, cn, e1, e2)


def _final_body(x2_ref, pT_ref, g_ref, o_ref):
    o_ref[...] = _rms(x2_ref[...] + pT_ref[...].T, g_ref[...])


def _final(x2, peerT, g):
    t = x2.shape[0]
    tm = TM_FINAL
    row = lambda i: (i, 0)
    return pl.pallas_call(
        _final_body,
        grid=(t // tm,),
        in_specs=[pl.BlockSpec((tm, D_MODEL), row), pl.BlockSpec((D_MODEL, tm), lambda i: (0, i)),
                  pl.BlockSpec((1, D_MODEL), lambda i: (0, 0))],
        out_specs=pl.BlockSpec((tm, D_MODEL), row),
        out_shape=jax.ShapeDtypeStruct((t, D_MODEL), F32),
        compiler_params=_params("parallel"),
        name="final",
    )(x2, peerT, g)


def kernel(x, norm_mix_g, w_in, conv_w, norm_conv_out_g, norm_attn_out_g, w_out, norm_ffn_g,
           peer_w_query, peer_sub_keys1, peer_sub_keys2, peer_down, peer_up, final_norm_g):
    batch, seq, d = x.shape
    assert w_in.shape[0] == 1, "one layer: the residual stream between layers is not implemented"
    assert d == D_MODEL and seq % MOBA_BLOCK == 0 and seq % TM_MIX == 0 and seq % TM_INPROJ == 0
    assert seq // MOBA_BLOCK > MOBA_TOPK
    x2d = x.reshape(batch * seq, d)
    bgate, u, qT, k, vT = _inproj(x2d, norm_mix_g[0].reshape(1, D_MODEL), w_in[0].astype(BF16),
                                  batch, seq)
    yT = _moba(qT, k, vT)
    x2, xnT = _mix(x2d, bgate, u, yT, conv_w[0], norm_conv_out_g[0].reshape(1, D_CONV),
                   norm_attn_out_g[0].reshape(1, D_ATTN), w_out[0].astype(BF16),
                   norm_ffn_g[0].reshape(1, D_MODEL), seq)
    rk, cn, e1, e2 = _prep(xnT, peer_w_query[0].T.astype(BF16),
                           peer_sub_keys1[0].astype(BF16), peer_sub_keys2[0].astype(BF16))
    outT = _peer(xnT, peer_down[0].astype(BF16), peer_up[0].T.astype(BF16), rk, cn, e1, e2)
    out = _final(x2, outT, final_norm_g.reshape(1, D_MODEL))
    return out.reshape(batch, seq, d)
```

```python
import functools

import numpy as np
import jax
import jax.numpy as jnp
from jax import lax
from jax.experimental import pallas as pl
from jax.experimental.pallas import tpu as pltpu

F32 = jnp.float32
BF16 = jnp.bfloat16

D_MODEL = 1024
D_CONV = 512
D_ATTN = 512
D_PROJ = 3 * D_CONV + 3 * D_ATTN
CONV_WIDTH = 3
ATTN_HEADS = 8
HEAD_DIM = 64
MOBA_BLOCK = 256
MOBA_TOPK = 3
MOBA_HEADS_PER_STEP = 2
PEER_HEADS = 8
PEER_NKEYS = 128
PEER_N = PEER_NKEYS * PEER_NKEYS
PEER_DKEY = 256
PEER_HALF = PEER_DKEY // 2
PEER_TOPK = 16
EPS = 1e-6
NEG = -1e30
GELU_HALF = 0.5

VMEM_LIMIT_BYTES = 48 * 1024 * 1024
PEER_VMEM_LIMIT_BYTES = 56 * 1024 * 1024
SUBLANES = 8

TM_INPROJ = 512
TM_MIX = 512
TM_PREP = 256
TM_PEER = 512
PEER_ROWS_PER_CHUNK = 1
PEER_CHUNK = PEER_ROWS_PER_CHUNK * PEER_NKEYS
PEER_CHUNKS_PER_STEP = 16
PEER_EB = PEER_CHUNKS_PER_STEP * PEER_CHUNK
BF16_ROWS = 16
PEER_LANE_TILE = 256
PEER_GATE_LANES = 256
PEER_SLAB_LANES = 128
TM_FINAL = 512

_CAND_PAIRS = [(r1, r2) for r1 in range(PEER_TOPK) for r2 in range(PEER_TOPK)
               if (r1 + 1) * (r2 + 1) <= PEER_TOPK]


def _params(*sem, vmem_limit_bytes=VMEM_LIMIT_BYTES):
    return pltpu.CompilerParams(dimension_semantics=sem, vmem_limit_bytes=vmem_limit_bytes)


def _rms(x, g):
    return x * lax.rsqrt(jnp.mean(x * x, axis=-1, keepdims=True) + EPS) * g


def _inproj_body(x_ref, g_ref, w_ref, b_ref, u_ref, q_ref, k_ref, v_ref):
    h = _rms(x_ref[...], g_ref[...]).astype(BF16)
    proj = jnp.dot(h, w_ref[...], preferred_element_type=F32)
    b_ref[...] = proj[:, 0:D_CONV]
    u_ref[...] = proj[:, D_CONV:2 * D_CONV] * proj[:, 2 * D_CONV:3 * D_CONV]
    o = 3 * D_CONV
    q_ref[...] = proj[:, o:o + D_ATTN].T.astype(BF16)
    k_ref[...] = proj[:, o + D_ATTN:o + 2 * D_ATTN].astype(BF16)
    v_ref[...] = proj[:, o + 2 * D_ATTN:o + 3 * D_ATTN].T.astype(BF16)


def _inproj(x2d, g, w_in_b, batch, seq):
    t = x2d.shape[0]
    tm = TM_INPROJ
    tps = seq // tm
    row = lambda i: (i, 0)
    fixed = lambda i: (0, 0)
    chan = lambda i: (i // tps, 0, i % tps)
    return pl.pallas_call(
        _inproj_body,
        grid=(t // tm,),
        in_specs=[pl.BlockSpec((tm, D_MODEL), row),
                  pl.BlockSpec((1, D_MODEL), fixed),
                  pl.BlockSpec((D_MODEL, D_PROJ), fixed)],
        out_specs=[pl.BlockSpec((tm, D_CONV), row), pl.BlockSpec((tm, D_CONV), row),
                   pl.BlockSpec((None, D_ATTN, tm), chan), pl.BlockSpec((tm, D_ATTN), row),
                   pl.BlockSpec((None, D_ATTN, tm), chan)],
        out_shape=[jax.ShapeDtypeStruct((t, D_CONV), F32), jax.ShapeDtypeStruct((t, D_CONV), F32),
                   jax.ShapeDtypeStruct((batch, D_ATTN, seq), BF16),
                   jax.ShapeDtypeStruct((t, D_ATTN), BF16),
                   jax.ShapeDtypeStruct((batch, D_ATTN, seq), BF16)],
        compiler_params=_params("parallel"),
        name="inproj",
    )(x2d, g, w_in_b)


def _moba_head(qT, k_ref, vT_ref, o_ref, rows, seq):
    nb = seq // MOBA_BLOCK
    bs = MOBA_BLOCK
    scale = float(1.0 / np.sqrt(HEAD_DIM))
    blk = lax.broadcasted_iota(jnp.int32, (nb, seq), 0)
    own = lax.shift_right_logical(lax.broadcasted_iota(jnp.int32, (nb, seq), 1),
                                  int(np.log2(bs)))
    avg = jnp.where(own == blk, 1.0 / bs, 0.0).astype(BF16)
    kmean = jnp.dot(avg, k_ref[...], preferred_element_type=F32)
    gate = jnp.dot(kmean.astype(BF16), qT, preferred_element_type=F32)
    past = blk < own
    g1 = jnp.where(past, gate, NEG)
    thr = g1
    for _ in range(MOBA_TOPK - 1):
        m = jnp.max(thr, axis=0, keepdims=True)
        thr = jnp.where(thr == m, NEG, thr)
    thr = jnp.max(thr, axis=0, keepdims=True)
    sel = jnp.where(jnp.logical_and(past, g1 >= thr), 1.0, 0.0)

    kpos = lax.broadcasted_iota(jnp.int32, (bs, bs), 0)
    qpos = lax.broadcasted_iota(jnp.int32, (bs, bs), 1)
    causal = kpos <= qpos
    for i in range(nb):
        n = (i + 1) * bs
        lanes = slice(i * bs, (i + 1) * bs)
        sT = jnp.dot(k_ref[0:n, :], qT[:, lanes], preferred_element_type=F32) * scale
        pieces = []
        for j in range(i):
            keep = sel[j:j + 1, lanes] > 0.5
            pieces.append(jnp.where(keep, sT[j * bs:(j + 1) * bs], NEG))
        pieces.append(jnp.where(causal, sT[i * bs:n], NEG))
        s = pieces[0] if i == 0 else jnp.concatenate(pieces, axis=0)
        m = jnp.max(s, axis=0, keepdims=True)
        p = jnp.exp(s - m)
        l = jnp.sum(p, axis=0, keepdims=True)
        oT = jnp.dot(vT_ref[rows, 0:n], p.astype(BF16), preferred_element_type=F32)
        o_ref[rows, lanes] = oT / l


def _moba_body(qT_ref, k_ref, vT_ref, o_ref, *, seq):
    chan = lax.broadcasted_iota(jnp.int32, (MOBA_HEADS_PER_STEP * HEAD_DIM, seq), 0)
    q2 = qT_ref[...]
    for hh in range(MOBA_HEADS_PER_STEP):
        rows = slice(hh * HEAD_DIM, (hh + 1) * HEAD_DIM)
        mine = jnp.logical_and(chan >= rows.start, chan < rows.stop)
        _moba_head(jnp.where(mine, q2, jnp.zeros((), BF16)), k_ref, vT_ref, o_ref, rows, seq)


def _moba(qT, k, vT):
    b, d, s = qT.shape
    w = MOBA_HEADS_PER_STEP * HEAD_DIM
    chan = lambda i, j: (i, j, 0)
    return pl.pallas_call(
        functools.partial(_moba_body, seq=s),
        grid=(b, d // w),
        in_specs=[pl.BlockSpec((None, w, s), chan),
                  pl.BlockSpec((s, w), lambda i, j: (i, j)),
                  pl.BlockSpec((None, w, s), chan)],
        out_specs=pl.BlockSpec((None, w, s), chan),
        out_shape=jax.ShapeDtypeStruct((b, d, s), F32),
        compiler_params=_params("parallel", "parallel"),
        name="moba",
    )(qT, k, vT)


def _mix_body(x_ref, b_ref, u_ref, uh_ref, ya_ref, cw_ref, gc_ref, ga_ref, wo_ref, gf_ref,
              x2_ref, xn_ref, uext_ref, *, tiles_per_seq):
    tm = u_ref.shape[0]
    first = (pl.program_id(0) % tiles_per_seq) == 0
    uext_ref[0:SUBLANES, :] = jnp.where(first, 0.0, uh_ref[...])
    u = u_ref[...]
    uext_ref[SUBLANES:SUBLANES + tm, :] = u
    cw = cw_ref[...]
    y = (cw[2:3, :] * u
         + cw[1:2, :] * uext_ref[SUBLANES - 1:SUBLANES - 1 + tm, :]
         + cw[0:1, :] * uext_ref[SUBLANES - 2:SUBLANES - 2 + tm, :])
    rc = _rms(b_ref[...] * y, gc_ref[...]).astype(BF16)
    ra = _rms(ya_ref[...].T, ga_ref[...]).astype(BF16)
    mix = (jnp.dot(rc, wo_ref[0:D_CONV, :], preferred_element_type=F32)
           + jnp.dot(ra, wo_ref[D_CONV:D_CONV + D_ATTN, :], preferred_element_type=F32))
    x2 = x_ref[...] + mix
    x2_ref[...] = x2
    xn_ref[...] = _rms(x2, gf_ref[...]).T.astype(BF16)


def _mix(x2d, bgate, u, yattnT, conv_w, gc, ga, w_out_b, gf, seq):
    t = x2d.shape[0]
    tm = TM_MIX
    tps = seq // tm
    row = lambda i: (i, 0)
    fixed = lambda i: (0, 0)
    halo = lambda i: (jnp.maximum(i * (tm // SUBLANES) - 1, 0), 0)
    return pl.pallas_call(
        functools.partial(_mix_body, tiles_per_seq=tps),
        grid=(t // tm,),
        in_specs=[pl.BlockSpec((tm, D_MODEL), row),
                  pl.BlockSpec((tm, D_CONV), row),
                  pl.BlockSpec((tm, D_CONV), row),
                  pl.BlockSpec((SUBLANES, D_CONV), halo),
                  pl.BlockSpec((None, D_ATTN, tm), lambda i: (i // tps, 0, i % tps)),
                  pl.BlockSpec((CONV_WIDTH, D_CONV), fixed),
                  pl.BlockSpec((1, D_CONV), fixed),
                  pl.BlockSpec((1, D_ATTN), fixed),
                  pl.BlockSpec((D_MODEL, D_MODEL), fixed),
                  pl.BlockSpec((1, D_MODEL), fixed)],
        out_specs=[pl.BlockSpec((tm, D_MODEL), row), pl.BlockSpec((D_MODEL, tm), lambda i: (0, i))],
        out_shape=[jax.ShapeDtypeStruct((t, D_MODEL), F32),
                   jax.ShapeDtypeStruct((D_MODEL, t), BF16)],
        scratch_shapes=[pltpu.VMEM((tm + SUBLANES, D_CONV), F32)],
        compiler_params=_params("parallel"),
        name="mix",
    )(x2d, bgate, u, u, yattnT, conv_w, gc, ga, w_out_b, gf)


def _oddeven_merge_sort_network(n):
    pairs = []
    p = 1
    while p < n:
        k = p
        while k >= 1:
            for j in range(k % p, n - k, 2 * k):
                for i in range(min(k, n - j - k)):
                    if (i + j) // (2 * p) == (i + j + k) // (2 * p):
                        pairs.append((i + j, i + j + k))
            k //= 2
        p *= 2
    return pairs


_SORT_NET = _oddeven_merge_sort_network(PEER_TOPK)


def _exchange(g, i, j):
    g[i], g[j] = jnp.maximum(g[i], g[j]), jnp.minimum(g[i], g[j])


def _slabs(s):
    return [s[SUBLANES * i:SUBLANES * (i + 1), :] for i in range(s.shape[0] // SUBLANES)]


def _sorted_top(slabs):
    n = PEER_TOPK
    g = list(slabs)
    assert len(g) == n
    for i, j in _SORT_NET:
        _exchange(g, i, j)
    shift = SUBLANES // 2
    while shift >= 1:
        rolled = [pltpu.roll(x, shift, axis=0) for x in g]
        g = [jnp.maximum(g[i], rolled[n - 1 - i]) for i in range(n)]
        d = n // 2
        while d >= 1:
            for i in range(n):
                if i & d == 0:
                    _exchange(g, i, i + d)
            d //= 2
        shift //= 2
    return g


def _kth_largest_packed(cands, k):
    sub = lax.broadcasted_iota(jnp.int32, cands[0].shape, 0)
    packed = []
    for q in range(0, len(cands), SUBLANES):
        group = cands[q:q + SUBLANES]
        x = group[0] if len(group) == SUBLANES else jnp.full_like(cands[0], NEG)
        for p, cnd in enumerate(group):
            if p or len(group) < SUBLANES:
                x = jnp.where(sub == p, cnd, x)
        packed.append(x)
    out = []
    for _ in range(k):
        m = functools.reduce(jnp.maximum, packed)
        m = jnp.max(m, axis=0, keepdims=True)
        out.append(jnp.broadcast_to(m, cands[0].shape))
        packed = [jnp.where(x == m, NEG, x) for x in packed]
    return out


def _prep_head(s1, s2):
    last = PEER_TOPK - 1
    g1, g2 = _slabs(s1), _slabs(s2)
    v1, v2 = _sorted_top(g1), _sorted_top(g2)
    cands = [v1[r1] + v2[r2] for r1, r2 in _CAND_PAIRS]
    tops = _kth_largest_packed(cands, PEER_TOPK)
    tau = tops[last]
    z = functools.reduce(jnp.add, [jnp.exp(t - tops[0]) for t in tops])
    cs = [jnp.zeros_like(tau) for _ in range(PEER_TOPK)]
    for cnd, (r1, _) in zip(cands, _CAND_PAIRS):
        cs[r1] = cs[r1] + jnp.where(cnd >= tau, 1.0, 0.0)
    scale = GELU_HALF / z
    rank2, cnt, e1, e2 = [], [], [], []
    for a, b in zip(g1, g2):
        c = jnp.zeros_like(a)
        for r in range(PEER_TOPK):
            c = jnp.where(a == v1[r], cs[r], c)
        cnt.append(c)
        rk = jnp.full_like(b, float(PEER_TOPK))
        for r in range(last, -1, -1):
            rk = jnp.where(b >= v2[r], float(r), rk)
        rank2.append(rk)
        e1.append(jnp.where(a >= v1[last], jnp.exp(a - v1[0]), 0.0) * scale)
        e2.append(jnp.where(b >= v2[last], jnp.exp(b - v2[0]), 0.0))
    cat = lambda parts: jnp.concatenate(parts, axis=0)
    return cat(rank2), cat(cnt), cat(e1), cat(e2)


def _prep_body(xnT_ref, wqT_ref, k1_ref, k2_ref, rk_ref, cn_ref, e1_ref, e2_ref, qry_ref):
    qry_ref[...] = jnp.dot(wqT_ref[...], xnT_ref[...], preferred_element_type=F32)
    for h in range(PEER_HEADS):
        o = h * PEER_DKEY
        q1 = qry_ref[o:o + PEER_HALF, :].astype(BF16)
        q2 = qry_ref[o + PEER_HALF:o + PEER_DKEY, :].astype(BF16)
        s1 = jnp.dot(k1_ref[h], q1, preferred_element_type=F32)
        s2 = jnp.dot(k2_ref[h], q2, preferred_element_type=F32)
        rank2, cnt, e1, e2 = _prep_head(s1, s2)
        rk_ref[h] = rank2.astype(BF16)
        e2_ref[h] = e2.astype(BF16)
        for lt in range(s1.shape[1] // PEER_SLAB_LANES):
            lanes = slice(lt * PEER_SLAB_LANES, (lt + 1) * PEER_SLAB_LANES)
            cn_ref[h, lt] = cnt[:, lanes]
            e1_ref[h, lt] = e1[:, lanes]


def _prep(xnT, wqT_b, k1_b, k2_b):
    t = xnT.shape[1]
    tm = TM_PREP
    col = lambda i: (0, i)
    col3 = lambda i: (0, 0, i)
    slab = lambda i: (0, i, 0, 0)
    hk = (PEER_HEADS, PEER_NKEYS, tm)
    hk_slab = (PEER_HEADS, tm // PEER_SLAB_LANES, PEER_NKEYS, PEER_SLAB_LANES)
    slab_shape = jax.ShapeDtypeStruct((PEER_HEADS, t // PEER_SLAB_LANES, PEER_NKEYS, PEER_SLAB_LANES), F32)
    return pl.pallas_call(
        _prep_body,
        grid=(t // tm,),
        in_specs=[pl.BlockSpec((D_MODEL, tm), col),
                  pl.BlockSpec((PEER_HEADS * PEER_DKEY, D_MODEL), lambda i: (0, 0)),
                  pl.BlockSpec((PEER_HEADS, PEER_NKEYS, PEER_HALF), lambda i: (0, 0, 0)),
                  pl.BlockSpec((PEER_HEADS, PEER_NKEYS, PEER_HALF), lambda i: (0, 0, 0))],
        out_specs=[pl.BlockSpec(hk, col3), pl.BlockSpec(hk_slab, slab), pl.BlockSpec(hk_slab, slab),
                   pl.BlockSpec(hk, col3)],
        out_shape=[jax.ShapeDtypeStruct((PEER_HEADS, PEER_NKEYS, t), BF16), slab_shape, slab_shape,
                   jax.ShapeDtypeStruct((PEER_HEADS, PEER_NKEYS, t), BF16)],
        scratch_shapes=[pltpu.VMEM((PEER_HEADS * PEER_DKEY, tm), F32)],
        compiler_params=_params("parallel"),
        name="prep",
    )(xnT, wqT_b, k1_b, k2_b)


def _peer_hidden(c, xnT_ref, dn_ref, h_ref):
    rows = slice(c * PEER_CHUNK, (c + 1) * PEER_CHUNK)
    h_ref[c] = jnp.dot(dn_ref[rows, :], xnT_ref[...], preferred_element_type=F32)


def _row_tile(ref, h, lt, i1):
    per = PEER_GATE_LANES // PEER_SLAB_LANES
    parts = [ref[h, lt * per + k, pl.ds(i1, BF16_ROWS, stride=0), :] for k in range(per)]
    return jnp.concatenate(parts, axis=1).astype(BF16)


def _peer_gate_chunk(blk, c, h_ref, rk_ref, cn_ref, e1_ref, e2_ref, a_ref):
    tm = a_ref.shape[1]
    sqrt_half = float(np.sqrt(0.5))
    slabs = PEER_NKEYS // BF16_ROWS
    zero = jnp.zeros((), BF16)
    for r in range(PEER_ROWS_PER_CHUNK):
        i1 = (blk * PEER_CHUNKS_PER_STEP + c) * PEER_ROWS_PER_CHUNK + r
        hr = h_ref[c, r * PEER_NKEYS:(r + 1) * PEER_NKEYS, :]
        act = (hr * (1.0 + lax.erf(hr * sqrt_half))).astype(BF16)
        for lt in range(tm // PEER_GATE_LANES):
            lanes = slice(lt * PEER_GATE_LANES, (lt + 1) * PEER_GATE_LANES)
            w = [None] * slabs
            for h in range(PEER_HEADS):
                c16 = _row_tile(cn_ref, h, lt, i1)
                e16 = _row_tile(e1_ref, h, lt, i1)
                for s in range(slabs):
                    sl = slice(s * BF16_ROWS, (s + 1) * BF16_ROWS)
                    term = jnp.where(rk_ref[h, sl, lanes] < c16, e2_ref[h, sl, lanes] * e16, zero)
                    w[s] = term if w[s] is None else w[s] + term
            base = c * PEER_CHUNK + r * PEER_NKEYS
            for s in range(slabs):
                sl = slice(s * BF16_ROWS, (s + 1) * BF16_ROWS)
                a_ref[base + s * BF16_ROWS:base + (s + 1) * BF16_ROWS, lanes] = w[s] * act[sl, lanes]


def _peer_up_piece(p, upT_ref, a_ref, o_ref):
    lane_tiles = o_ref.shape[1] // PEER_LANE_TILE
    band = o_ref.shape[0] // (PEER_CHUNKS_PER_STEP // lane_tiles)
    rows = slice((p // lane_tiles) * band, (p // lane_tiles + 1) * band)
    lanes = slice((p % lane_tiles) * PEER_LANE_TILE, (p % lane_tiles + 1) * PEER_LANE_TILE)
    o_ref[rows, lanes] += jnp.dot(upT_ref[rows, :], a_ref[:, lanes], preferred_element_type=F32)


def _peer_step(blk, xnT_ref, dn_ref, upT_ref, rk_ref, cn_ref, e1_ref, e2_ref, o_ref, h_ref,
               a_out, a_in):
    nch = PEER_CHUNKS_PER_STEP
    assert nch % (o_ref.shape[1] // PEER_LANE_TILE) == 0
    if a_out is not None:
        _peer_hidden(0, xnT_ref, dn_ref, h_ref)
    for c in range(nch):
        if a_out is not None:
            if c + 1 < nch:
                _peer_hidden(c + 1, xnT_ref, dn_ref, h_ref)
            _peer_gate_chunk(blk, c, h_ref, rk_ref, cn_ref, e1_ref, e2_ref, a_out)
        if a_in is not None:
            _peer_up_piece(c, upT_ref, a_in, o_ref)


def _peer_body(xnT_ref, dn_ref, upT_ref, rk_ref, cn_ref, e1_ref, e2_ref, o_ref, a0_ref, a1_ref,
               h_ref):
    j = pl.program_id(1)
    nblk = PEER_N // PEER_EB
    step = functools.partial(_peer_step, j, xnT_ref, dn_ref, upT_ref, rk_ref, cn_ref, e1_ref,
                             e2_ref, o_ref, h_ref)
    even = lax.rem(j, 2) == 0
    inner = jnp.logical_and(j > 0, j < nblk)

    @pl.when(j == 0)
    def _():
        o_ref[...] = jnp.zeros_like(o_ref)
        step(a0_ref, None)

    @pl.when(jnp.logical_and(inner, even))
    def _():
        step(a0_ref, a1_ref)

    @pl.when(jnp.logical_and(inner, jnp.logical_not(even)))
    def _():
        step(a1_ref, a0_ref)

    @pl.when(j == nblk)
    def _():
        step(None, a1_ref if (nblk - 1) % 2 else a0_ref)


def _peer(xnT, down_b, upT_b, rk, cn, e1, e2):
    t = xnT.shape[1]
    tm = TM_PEER
    nblk = PEER_N // PEER_EB
    hk = (PEER_HEADS, PEER_NKEYS, tm)
    hk_slab = (PEER_HEADS, tm // PEER_SLAB_LANES, PEER_NKEYS, PEER_SLAB_LANES)
    tok3 = lambda i, j: (0, 0, i)
    slab = lambda i, j: (0, i, 0, 0)
    return pl.pallas_call(
        _peer_body,
        grid=(t // tm, nblk + 1),
        in_specs=[pl.BlockSpec((D_MODEL, tm), lambda i, j: (0, i)),
                  pl.BlockSpec((PEER_EB, D_MODEL), lambda i, j: (jnp.minimum(j, nblk - 1), 0)),
                  pl.BlockSpec((D_MODEL, PEER_EB), lambda i, j: (0, jnp.maximum(j - 1, 0))),
                  pl.BlockSpec(hk, tok3), pl.BlockSpec(hk_slab, slab),
                  pl.BlockSpec(hk_slab, slab), pl.BlockSpec(hk, tok3)],
        out_specs=pl.BlockSpec((D_MODEL, tm), lambda i, j: (0, i)),
        out_shape=jax.ShapeDtypeStruct((D_MODEL, t), F32),
        scratch_shapes=[pltpu.VMEM((PEER_EB, tm), BF16), pltpu.VMEM((PEER_EB, tm), BF16),
                        pltpu.VMEM((PEER_CHUNKS_PER_STEP, PEER_CHUNK, tm), F32)],
        compiler_params=_params("parallel", "arbitrary", vmem_limit_bytes=PEER_VMEM_LIMIT_BYTES),
        name="peer",
    )(xnT, down_b, upT_b, rk, cn, e1, e2)


def _final_body(x2_ref, pT_ref, g_ref, o_ref):
    o_ref[...] = _rms(x2_ref[...] + pT_ref[...].T, g_ref[...])


def _final(x2, peerT, g):
    t = x2.shape[0]
    tm = TM_FINAL
    row = lambda i: (i, 0)
    return pl.pallas_call(
        _final_body,
        grid=(t // tm,),
        in_specs=[pl.BlockSpec((tm, D_MODEL), row), pl.BlockSpec((D_MODEL, tm), lambda i: (0, i)),
                  pl.BlockSpec((1, D_MODEL), lambda i: (0, 0))],
        out_specs=pl.BlockSpec((tm, D_MODEL), row),
        out_shape=jax.ShapeDtypeStruct((t, D_MODEL), F32),
        compiler_params=_params("parallel"),
        name="final",
    )(x2, peerT, g)


def kernel(x, norm_mix_g, w_in, conv_w, norm_conv_out_g, norm_attn_out_g, w_out, norm_ffn_g,
           peer_w_query, peer_sub_keys1, peer_sub_keys2, peer_down, peer_up, final_norm_g):
    batch, seq, d = x.shape
    assert w_in.shape[0] == 1, "one layer: the residual stream between layers is not implemented"
    assert d == D_MODEL and seq % MOBA_BLOCK == 0 and seq % TM_MIX == 0 and seq % TM_INPROJ == 0
    assert seq // MOBA_BLOCK > MOBA_TOPK
    x2d = x.reshape(batch * seq, d)
    bgate, u, qT, k, vT = _inproj(x2d, norm_mix_g[0].reshape(1, D_MODEL), w_in[0].astype(BF16),
                                  batch, seq)
    yT = _moba(qT, k, vT)
    x2, xnT = _mix(x2d, bgate, u, yT, conv_w[0], norm_conv_out_g[0].reshape(1, D_CONV),
                   norm_attn_out_g[0].reshape(1, D_ATTN), w_out[0].astype(BF16),
                   norm_ffn_g[0].reshape(1, D_MODEL), seq)
    rk, cn, e1, e2 = _prep(xnT, peer_w_query[0].T.astype(BF16),
                           peer_sub_keys1[0].astype(BF16), peer_sub_keys2[0].astype(BF16))
    outT = _peer(xnT, peer_down[0].astype(BF16), peer_up[0].T.astype(BF16), rk, cn, e1, e2)
    out = _final(x2, outT, final_norm_g.reshape(1, D_MODEL))
    return out.reshape(batch, seq, d)
```

```python
import functools

import numpy as np
import jax
import jax.numpy as jnp
from jax import lax
from jax.experimental import pallas as pl
from jax.experimental.pallas import tpu as pltpu

F32 = jnp.float32
BF16 = jnp.bfloat16

D_MODEL = 1024
D_CONV = 512
D_ATTN = 512
D_PROJ = 3 * D_CONV + 3 * D_ATTN
CONV_WIDTH = 3
ATTN_HEADS = 8
HEAD_DIM = 64
MOBA_BLOCK = 256
MOBA_TOPK = 3
MOBA_HEADS_PER_STEP = 2
PEER_HEADS = 8
PEER_NKEYS = 128
PEER_N = PEER_NKEYS * PEER_NKEYS
PEER_DKEY = 256
PEER_HALF = PEER_DKEY // 2
PEER_TOPK = 16
EPS = 1e-6
NEG = -1e30
GELU_HALF = 0.5

VMEM_LIMIT_BYTES = 48 * 1024 * 1024
PEER_VMEM_LIMIT_BYTES = 56 * 1024 * 1024
SUBLANES = 8

TM_INPROJ = 512
TM_MIX = 512
TM_PREP = 256
TM_PEER = 512
PEER_ROWS_PER_CHUNK = 4
PEER_CHUNK = PEER_ROWS_PER_CHUNK * PEER_NKEYS
PEER_CHUNKS_PER_STEP = 4
PEER_EB = PEER_CHUNKS_PER_STEP * PEER_CHUNK
BF16_ROWS = 16
PEER_LANE_TILE = 256
PEER_GATE_LANES = 256
PEER_SLAB_LANES = 128
TM_FINAL = 512

_CAND_PAIRS = [(r1, r2) for r1 in range(PEER_TOPK) for r2 in range(PEER_TOPK)
               if (r1 + 1) * (r2 + 1) <= PEER_TOPK]


def _params(*sem, vmem_limit_bytes=VMEM_LIMIT_BYTES):
    return pltpu.CompilerParams(dimension_semantics=sem, vmem_limit_bytes=vmem_limit_bytes)


def _rms(x, g):
    return x * lax.rsqrt(jnp.mean(x * x, axis=-1, keepdims=True) + EPS) * g


def _inproj_body(x_ref, g_ref, w_ref, b_ref, u_ref, q_ref, k_ref, v_ref):
    h = _rms(x_ref[...], g_ref[...]).astype(BF16)
    proj = jnp.dot(h, w_ref[...], preferred_element_type=F32)
    b_ref[...] = proj[:, 0:D_CONV]
    u_ref[...] = proj[:, D_CONV:2 * D_CONV] * proj[:, 2 * D_CONV:3 * D_CONV]
    o = 3 * D_CONV
    q_ref[...] = proj[:, o:o + D_ATTN].T.astype(BF16)
    k_ref[...] = proj[:, o + D_ATTN:o + 2 * D_ATTN].astype(BF16)
    v_ref[...] = proj[:, o + 2 * D_ATTN:o + 3 * D_ATTN].T.astype(BF16)


def _inproj(x2d, g, w_in_b, batch, seq):
    t = x2d.shape[0]
    tm = TM_INPROJ
    tps = seq // tm
    row = lambda i: (i, 0)
    fixed = lambda i: (0, 0)
    chan = lambda i: (i // tps, 0, i % tps)
    return pl.pallas_call(
        _inproj_body,
        grid=(t // tm,),
        in_specs=[pl.BlockSpec((tm, D_MODEL), row),
                  pl.BlockSpec((1, D_MODEL), fixed),
                  pl.BlockSpec((D_MODEL, D_PROJ), fixed)],
        out_specs=[pl.BlockSpec((tm, D_CONV), row), pl.BlockSpec((tm, D_CONV), row),
                   pl.BlockSpec((None, D_ATTN, tm), chan), pl.BlockSpec((tm, D_ATTN), row),
                   pl.BlockSpec((None, D_ATTN, tm), chan)],
        out_shape=[jax.ShapeDtypeStruct((t, D_CONV), F32), jax.ShapeDtypeStruct((t, D_CONV), F32),
                   jax.ShapeDtypeStruct((batch, D_ATTN, seq), BF16),
                   jax.ShapeDtypeStruct((t, D_ATTN), BF16),
                   jax.ShapeDtypeStruct((batch, D_ATTN, seq), BF16)],
        compiler_params=_params("parallel"),
        name="inproj",
    )(x2d, g, w_in_b)


def _moba_head(qT, k_ref, vT_ref, o_ref, rows, seq):
    nb = seq // MOBA_BLOCK
    bs = MOBA_BLOCK
    scale = float(1.0 / np.sqrt(HEAD_DIM))
    blk = lax.broadcasted_iota(jnp.int32, (nb, seq), 0)
    own = lax.shift_right_logical(lax.broadcasted_iota(jnp.int32, (nb, seq), 1),
                                  int(np.log2(bs)))
    avg = jnp.where(own == blk, 1.0 / bs, 0.0).astype(BF16)
    kmean = jnp.dot(avg, k_ref[...], preferred_element_type=F32)
    gate = jnp.dot(kmean.astype(BF16), qT, preferred_element_type=F32)
    past = blk < own
    g1 = jnp.where(past, gate, NEG)
    thr = g1
    for _ in range(MOBA_TOPK - 1):
        m = jnp.max(thr, axis=0, keepdims=True)
        thr = jnp.where(thr == m, NEG, thr)
    thr = jnp.max(thr, axis=0, keepdims=True)
    sel = jnp.where(jnp.logical_and(past, g1 >= thr), 1.0, 0.0)

    kpos = lax.broadcasted_iota(jnp.int32, (bs, bs), 0)
    qpos = lax.broadcasted_iota(jnp.int32, (bs, bs), 1)
    causal = kpos <= qpos
    for i in range(nb):
        n = (i + 1) * bs
        lanes = slice(i * bs, (i + 1) * bs)
        sT = jnp.dot(k_ref[0:n, :], qT[:, lanes], preferred_element_type=F32) * scale
        pieces = []
        for j in range(i):
            keep = sel[j:j + 1, lanes] > 0.5
            pieces.append(jnp.where(keep, sT[j * bs:(j + 1) * bs], NEG))
        pieces.append(jnp.where(causal, sT[i * bs:n], NEG))
        s = pieces[0] if i == 0 else jnp.concatenate(pieces, axis=0)
        m = jnp.max(s, axis=0, keepdims=True)
        p = jnp.exp(s - m)
        l = jnp.sum(p, axis=0, keepdims=True)
        oT = jnp.dot(vT_ref[rows, 0:n], p.astype(BF16), preferred_element_type=F32)
        o_ref[rows, lanes] = oT / l


def _moba_body(qT_ref, k_ref, vT_ref, o_ref, *, seq):
    chan = lax.broadcasted_iota(jnp.int32, (MOBA_HEADS_PER_STEP * HEAD_DIM, seq), 0)
    q2 = qT_ref[...]
    for hh in range(MOBA_HEADS_PER_STEP):
        rows = slice(hh * HEAD_DIM, (hh + 1) * HEAD_DIM)
        mine = jnp.logical_and(chan >= rows.start, chan < rows.stop)
        _moba_head(jnp.where(mine, q2, jnp.zeros((), BF16)), k_ref, vT_ref, o_ref, rows, seq)


def _moba(qT, k, vT):
    b, d, s = qT.shape
    w = MOBA_HEADS_PER_STEP * HEAD_DIM
    chan = lambda i, j: (i, j, 0)
    return pl.pallas_call(
        functools.partial(_moba_body, seq=s),
        grid=(b, d // w),
        in_specs=[pl.BlockSpec((None, w, s), chan),
                  pl.BlockSpec((s, w), lambda i, j: (i, j)),
                  pl.BlockSpec((None, w, s), chan)],
        out_specs=pl.BlockSpec((None, w, s), chan),
        out_shape=jax.ShapeDtypeStruct((b, d, s), F32),
        compiler_params=_params("parallel", "parallel"),
        name="moba",
    )(qT, k, vT)


def _mix_body(x_ref, b_ref, u_ref, uh_ref, ya_ref, cw_ref, gc_ref, ga_ref, wo_ref, gf_ref,
              x2_ref, xn_ref, uext_ref, *, tiles_per_seq):
    tm = u_ref.shape[0]
    first = (pl.program_id(0) % tiles_per_seq) == 0
    uext_ref[0:SUBLANES, :] = jnp.where(first, 0.0, uh_ref[...])
    u = u_ref[...]
    uext_ref[SUBLANES:SUBLANES + tm, :] = u
    cw = cw_ref[...]
    y = (cw[2:3, :] * u
         + cw[1:2, :] * uext_ref[SUBLANES - 1:SUBLANES - 1 + tm, :]
         + cw[0:1, :] * uext_ref[SUBLANES - 2:SUBLANES - 2 + tm, :])
    rc = _rms(b_ref[...] * y, gc_ref[...]).astype(BF16)
    ra = _rms(ya_ref[...].T, ga_ref[...]).astype(BF16)
    mix = (jnp.dot(rc, wo_ref[0:D_CONV, :], preferred_element_type=F32)
           + jnp.dot(ra, wo_ref[D_CONV:D_CONV + D_ATTN, :], preferred_element_type=F32))
    x2 = x_ref[...] + mix
    x2_ref[...] = x2
    xn_ref[...] = _rms(x2, gf_ref[...]).T.astype(BF16)


def _mix(x2d, bgate, u, yattnT, conv_w, gc, ga, w_out_b, gf, seq):
    t = x2d.shape[0]
    tm = TM_MIX
    tps = seq // tm
    row = lambda i: (i, 0)
    fixed = lambda i: (0, 0)
    halo = lambda i: (jnp.maximum(i * (tm // SUBLANES) - 1, 0), 0)
    return pl.pallas_call(
        functools.partial(_mix_body, tiles_per_seq=tps),
        grid=(t // tm,),
        in_specs=[pl.BlockSpec((tm, D_MODEL), row),
                  pl.BlockSpec((tm, D_CONV), row),
                  pl.BlockSpec((tm, D_CONV), row),
                  pl.BlockSpec((SUBLANES, D_CONV), halo),
                  pl.BlockSpec((None, D_ATTN, tm), lambda i: (i // tps, 0, i % tps)),
                  pl.BlockSpec((CONV_WIDTH, D_CONV), fixed),
                  pl.BlockSpec((1, D_CONV), fixed),
                  pl.BlockSpec((1, D_ATTN), fixed),
                  pl.BlockSpec((D_MODEL, D_MODEL), fixed),
                  pl.BlockSpec((1, D_MODEL), fixed)],
        out_specs=[pl.BlockSpec((tm, D_MODEL), row), pl.BlockSpec((D_MODEL, tm), lambda i: (0, i))],
        out_shape=[jax.ShapeDtypeStruct((t, D_MODEL), F32),
                   jax.ShapeDtypeStruct((D_MODEL, t), BF16)],
        scratch_shapes=[pltpu.VMEM((tm + SUBLANES, D_CONV), F32)],
        compiler_params=_params("parallel"),
        name="mix",
    )(x2d, bgate, u, u, yattnT, conv_w, gc, ga, w_out_b, gf)


def _oddeven_merge_sort_network(n):
    pairs = []
    p = 1
    while p < n:
        k = p
        while k >= 1:
            for j in range(k % p, n - k, 2 * k):
                for i in range(min(k, n - j - k)):
                    if (i + j) // (2 * p) == (i + j + k) // (2 * p):
                        pairs.append((i + j, i + j + k))
            k //= 2
        p *= 2
    return pairs


_SORT_NET = _oddeven_merge_sort_network(PEER_TOPK)


def _exchange(g, i, j):
    g[i], g[j] = jnp.maximum(g[i], g[j]), jnp.minimum(g[i], g[j])


def _slabs(s):
    return [s[SUBLANES * i:SUBLANES * (i + 1), :] for i in range(s.shape[0] // SUBLANES)]


def _sorted_top(slabs):
    n = PEER_TOPK
    g = list(slabs)
    assert len(g) == n
    for i, j in _SORT_NET:
        _exchange(g, i, j)
    shift = SUBLANES // 2
    while shift >= 1:
        rolled = [pltpu.roll(x, shift, axis=0) for x in g]
        g = [jnp.maximum(g[i], rolled[n - 1 - i]) for i in range(n)]
        d = n // 2
        while d >= 1:
            for i in range(n):
                if i & d == 0:
                    _exchange(g, i, i + d)
            d //= 2
        shift //= 2
    return g


def _kth_largest_packed(cands, k):
    sub = lax.broadcasted_iota(jnp.int32, cands[0].shape, 0)
    packed = []
    for q in range(0, len(cands), SUBLANES):
        group = cands[q:q + SUBLANES]
        x = group[0] if len(group) == SUBLANES else jnp.full_like(cands[0], NEG)
        for p, cnd in enumerate(group):
            if p or len(group) < SUBLANES:
                x = jnp.where(sub == p, cnd, x)
        packed.append(x)
    out = []
    for _ in range(k):
        m = functools.reduce(jnp.maximum, packed)
        m = jnp.max(m, axis=0, keepdims=True)
        out.append(jnp.broadcast_to(m, cands[0].shape))
        packed = [jnp.where(x == m, NEG, x) for x in packed]
    return out


def _prep_head(s1, s2):
    last = PEER_TOPK - 1
    g1, g2 = _slabs(s1), _slabs(s2)
    v1, v2 = _sorted_top(g1), _sorted_top(g2)
    cands = [v1[r1] + v2[r2] for r1, r2 in _CAND_PAIRS]
    tops = _kth_largest_packed(cands, PEER_TOPK)
    tau = tops[last]
    z = functools.reduce(jnp.add, [jnp.exp(t - tops[0]) for t in tops])
    cs = [jnp.zeros_like(tau) for _ in range(PEER_TOPK)]
    for cnd, (r1, _) in zip(cands, _CAND_PAIRS):
        cs[r1] = cs[r1] + jnp.where(cnd >= tau, 1.0, 0.0)
    scale = GELU_HALF / z
    rank2, cnt, e1, e2 = [], [], [], []
    for a, b in zip(g1, g2):
        c = jnp.zeros_like(a)
        for r in range(PEER_TOPK):
            c = jnp.where(a == v1[r], cs[r], c)
        cnt.append(c)
        rk = jnp.full_like(b, float(PEER_TOPK))
        for r in range(last, -1, -1):
            rk = jnp.where(b >= v2[r], float(r), rk)
        rank2.append(rk)
        e1.append(jnp.where(a >= v1[last], jnp.exp(a - v1[0]), 0.0) * scale)
        e2.append(jnp.where(b >= v2[last], jnp.exp(b - v2[0]), 0.0))
    cat = lambda parts: jnp.concatenate(parts, axis=0)
    return cat(rank2), cat(cnt), cat(e1), cat(e2)


def _prep_body(xnT_ref, wqT_ref, k1_ref, k2_ref, rk_ref, cn_ref, e1_ref, e2_ref, qry_ref):
    qry_ref[...] = jnp.dot(wqT_ref[...], xnT_ref[...], preferred_element_type=F32)
    for h in range(PEER_HEADS):
        o = h * PEER_DKEY
        q1 = qry_ref[o:o + PEER_HALF, :].astype(BF16)
        q2 = qry_ref[o + PEER_HALF:o + PEER_DKEY, :].astype(BF16)
        s1 = jnp.dot(k1_ref[h], q1, preferred_element_type=F32)
        s2 = jnp.dot(k2_ref[h], q2, preferred_element_type=F32)
        rank2, cnt, e1, e2 = _prep_head(s1, s2)
        rk_ref[h] = rank2.astype(BF16)
        e2_ref[h] = e2.astype(BF16)
        for lt in range(s1.shape[1] // PEER_SLAB_LANES):
            lanes = slice(lt * PEER_SLAB_LANES, (lt + 1) * PEER_SLAB_LANES)
            cn_ref[h, lt] = cnt[:, lanes]
            e1_ref[h, lt] = e1[:, lanes]


def _prep(xnT, wqT_b, k1_b, k2_b):
    t = xnT.shape[1]
    tm = TM_PREP
    col = lambda i: (0, i)
    col3 = lambda i: (0, 0, i)
    slab = lambda i: (0, i, 0, 0)
    hk = (PEER_HEADS, PEER_NKEYS, tm)
    hk_slab = (PEER_HEADS, tm // PEER_SLAB_LANES, PEER_NKEYS, PEER_SLAB_LANES)
    slab_shape = jax.ShapeDtypeStruct((PEER_HEADS, t // PEER_SLAB_LANES, PEER_NKEYS, PEER_SLAB_LANES), F32)
    return pl.pallas_call(
        _prep_body,
        grid=(t // tm,),
        in_specs=[pl.BlockSpec((D_MODEL, tm), col),
                  pl.BlockSpec((PEER_HEADS * PEER_DKEY, D_MODEL), lambda i: (0, 0)),
                  pl.BlockSpec((PEER_HEADS, PEER_NKEYS, PEER_HALF), lambda i: (0, 0, 0)),
                  pl.BlockSpec((PEER_HEADS, PEER_NKEYS, PEER_HALF), lambda i: (0, 0, 0))],
        out_specs=[pl.BlockSpec(hk, col3), pl.BlockSpec(hk_slab, slab), pl.BlockSpec(hk_slab, slab),
                   pl.BlockSpec(hk, lambda i: (0, 0, jnp.bitwise_xor(i, 1)))],
        out_shape=[jax.ShapeDtypeStruct((PEER_HEADS, PEER_NKEYS, t), BF16), slab_shape, slab_shape,
                   jax.ShapeDtypeStruct((PEER_HEADS, PEER_NKEYS, t), BF16)],
        scratch_shapes=[pltpu.VMEM((PEER_HEADS * PEER_DKEY, tm), F32)],
        compiler_params=_params("parallel"),
        name="prep",
    )(xnT, wqT_b, k1_b, k2_b)


def _peer_hidden(c, xnT_ref, dn_ref, h_ref):
    rows = slice(c * PEER_CHUNK, (c + 1) * PEER_CHUNK)
    h_ref[c] = jnp.dot(dn_ref[rows, :], xnT_ref[...], preferred_element_type=F32)


def _swapped_lanes(lt):
    assert TM_PREP == PEER_GATE_LANES and (TM_PEER // PEER_GATE_LANES) % 2 == 0
    return slice((lt ^ 1) * PEER_GATE_LANES, ((lt ^ 1) + 1) * PEER_GATE_LANES)


def _row_tile(ref, h, lt, i1):
    per = PEER_GATE_LANES // PEER_SLAB_LANES
    parts = [ref[h, lt * per + k, pl.ds(i1, BF16_ROWS, stride=0), :] for k in range(per)]
    return jnp.concatenate(parts, axis=1).astype(BF16)


def _peer_gate_chunk(blk, c, h_ref, rk_ref, cn_ref, e1_ref, e2_ref, a_ref):
    tm = a_ref.shape[1]
    sqrt_half = float(np.sqrt(0.5))
    slabs = PEER_NKEYS // BF16_ROWS
    zero = jnp.zeros((), BF16)
    for r in range(PEER_ROWS_PER_CHUNK):
        i1 = (blk * PEER_CHUNKS_PER_STEP + c) * PEER_ROWS_PER_CHUNK + r
        hr = h_ref[c, r * PEER_NKEYS:(r + 1) * PEER_NKEYS, :]
        act = (hr * (1.0 + lax.erf(hr * sqrt_half))).astype(BF16)
        for lt in range(tm // PEER_GATE_LANES):
            lanes = slice(lt * PEER_GATE_LANES, (lt + 1) * PEER_GATE_LANES)
            w = [None] * slabs
            for h in range(PEER_HEADS):
                c16 = _row_tile(cn_ref, h, lt, i1)
                e16 = _row_tile(e1_ref, h, lt, i1)
                for s in range(slabs):
                    sl = slice(s * BF16_ROWS, (s + 1) * BF16_ROWS)
                    term = jnp.where(rk_ref[h, sl, lanes] < c16,
                                     e2_ref[h, sl, _swapped_lanes(lt)] * e16, zero)
                    w[s] = term if w[s] is None else w[s] + term
            base = c * PEER_CHUNK + r * PEER_NKEYS
            for s in range(slabs):
                sl = slice(s * BF16_ROWS, (s + 1) * BF16_ROWS)
                a_ref[base + s * BF16_ROWS:base + (s + 1) * BF16_ROWS, lanes] = w[s] * act[sl, lanes]


def _peer_up_piece(p, upT_ref, a_ref, o_ref):
    lane_tiles = o_ref.shape[1] // PEER_LANE_TILE
    band = o_ref.shape[0] // (PEER_CHUNKS_PER_STEP // lane_tiles)
    rows = slice((p // lane_tiles) * band, (p // lane_tiles + 1) * band)
    lanes = slice((p % lane_tiles) * PEER_LANE_TILE, (p % lane_tiles + 1) * PEER_LANE_TILE)
    o_ref[rows, lanes] += jnp.dot(upT_ref[rows, :], a_ref[:, lanes], preferred_element_type=F32)


def _peer_step(blk, xnT_ref, dn_ref, upT_ref, rk_ref, cn_ref, e1_ref, e2_ref, o_ref, h_ref,
               a_out, a_in):
    nch = PEER_CHUNKS_PER_STEP
    assert nch % (o_ref.shape[1] // PEER_LANE_TILE) == 0
    if a_out is not None:
        _peer_hidden(0, xnT_ref, dn_ref, h_ref)
    for c in range(nch):
        if a_out is not None:
            if c + 1 < nch:
                _peer_hidden(c + 1, xnT_ref, dn_ref, h_ref)
            _peer_gate_chunk(blk, c, h_ref, rk_ref, cn_ref, e1_ref, e2_ref, a_out)
        if a_in is not None:
            _peer_up_piece(c, upT_ref, a_in, o_ref)


def _peer_body(xnT_ref, dn_ref, upT_ref, rk_ref, cn_ref, e1_ref, e2_ref, o_ref, a0_ref, a1_ref,
               h_ref):
    j = pl.program_id(1)
    nblk = PEER_N // PEER_EB
    step = functools.partial(_peer_step, j, xnT_ref, dn_ref, upT_ref, rk_ref, cn_ref, e1_ref,
                             e2_ref, o_ref, h_ref)
    even = lax.rem(j, 2) == 0
    inner = jnp.logical_and(j > 0, j < nblk)

    @pl.when(j == 0)
    def _():
        o_ref[...] = jnp.zeros_like(o_ref)
        step(a0_ref, None)

    @pl.when(jnp.logical_and(inner, even))
    def _():
        step(a0_ref, a1_ref)

    @pl.when(jnp.logical_and(inner, jnp.logical_not(even)))
    def _():
        step(a1_ref, a0_ref)

    @pl.when(j == nblk)
    def _():
        step(None, a1_ref if (nblk - 1) % 2 else a0_ref)


def _peer(xnT, down_b, upT_b, rk, cn, e1, e2):
    t = xnT.shape[1]
    tm = TM_PEER
    nblk = PEER_N // PEER_EB
    hk = (PEER_HEADS, PEER_NKEYS, tm)
    hk_slab = (PEER_HEADS, tm // PEER_SLAB_LANES, PEER_NKEYS, PEER_SLAB_LANES)
    tok3 = lambda i, j: (0, 0, i)
    slab = lambda i, j: (0, i, 0, 0)
    return pl.pallas_call(
        _peer_body,
        grid=(t // tm, nblk + 1),
        in_specs=[pl.BlockSpec((D_MODEL, tm), lambda i, j: (0, i)),
                  pl.BlockSpec((PEER_EB, D_MODEL), lambda i, j: (jnp.minimum(j, nblk - 1), 0)),
                  pl.BlockSpec((D_MODEL, PEER_EB), lambda i, j: (0, jnp.maximum(j - 1, 0))),
                  pl.BlockSpec(hk, tok3), pl.BlockSpec(hk_slab, slab),
                  pl.BlockSpec(hk_slab, slab), pl.BlockSpec(hk, tok3)],
        out_specs=pl.BlockSpec((D_MODEL, tm), lambda i, j: (0, i)),
        out_shape=jax.ShapeDtypeStruct((D_MODEL, t), F32),
        scratch_shapes=[pltpu.VMEM((PEER_EB, tm), BF16), pltpu.VMEM((PEER_EB, tm), BF16),
                        pltpu.VMEM((PEER_CHUNKS_PER_STEP, PEER_CHUNK, tm), F32)],
        compiler_params=_params("parallel", "arbitrary", vmem_limit_bytes=PEER_VMEM_LIMIT_BYTES),
        name="peer",
    )(xnT, down_b, upT_b, rk, cn, e1, e2)


def _final_body(x2_ref, pT_ref, g_ref, o_ref):
    o_ref[...] = _rms(x2_ref[...] + pT_ref[...].T, g_ref[...])


def _final(x2, peerT, g):
    t = x2.shape[0]
    tm = TM_FINAL
    row = lambda i: (i, 0)
    return pl.pallas_call(
        _final_body,
        grid=(t // tm,),
        in_specs=[pl.BlockSpec((tm, D_MODEL), row), pl.BlockSpec((D_MODEL, tm), lambda i: (0, i)),
                  pl.BlockSpec((1, D_MODEL), lambda i: (0, 0))],
        out_specs=pl.BlockSpec((tm, D_MODEL), row),
        out_shape=jax.ShapeDtypeStruct((t, D_MODEL), F32),
        compiler_params=_params("parallel"),
        name="final",
    )(x2, peerT, g)


def kernel(x, norm_mix_g, w_in, conv_w, norm_conv_out_g, norm_attn_out_g, w_out, norm_ffn_g,
           peer_w_query, peer_sub_keys1, peer_sub_keys2, peer_down, peer_up, final_norm_g):
    batch, seq, d = x.shape
    assert w_in.shape[0] == 1, "one layer: the residual stream between layers is not implemented"
    assert d == D_MODEL and seq % MOBA_BLOCK == 0 and seq % TM_MIX == 0 and seq % TM_INPROJ == 0
    assert seq // MOBA_BLOCK > MOBA_TOPK
    x2d = x.reshape(batch * seq, d)
    bgate, u, qT, k, vT = _inproj(x2d, norm_mix_g[0].reshape(1, D_MODEL), w_in[0].astype(BF16),
                                  batch, seq)
    yT = _moba(qT, k, vT)
    x2, xnT = _mix(x2d, bgate, u, yT, conv_w[0], norm_conv_out_g[0].reshape(1, D_CONV),
                   norm_attn_out_g[0].reshape(1, D_ATTN), w_out[0].astype(BF16),
                   norm_ffn_g[0].reshape(1, D_MODEL), seq)
    rk, cn, e1, e2 = _prep(xnT, peer_w_query[0].T.astype(BF16),
                           peer_sub_keys1[0].astype(BF16), peer_sub_keys2[0].astype(BF16))
    outT = _peer(xnT, peer_down[0].astype(BF16), peer_up[0].T.astype(BF16), rk, cn, e1, e2)
    out = _final(x2, outT, final_norm_g.reshape(1, D_MODEL))
    return out.reshape(batch, seq, d)
```

```python
import functools

import numpy as np
import jax
import jax.numpy as jnp
from jax import lax
from jax.experimental import pallas as pl
from jax.experimental.pallas import tpu as pltpu

F32 = jnp.float32
BF16 = jnp.bfloat16

D_MODEL = 1024
D_CONV = 512
D_ATTN = 512
D_PROJ = 3 * D_CONV + 3 * D_ATTN
CONV_WIDTH = 3
ATTN_HEADS = 8
HEAD_DIM = 64
MOBA_BLOCK = 256
MOBA_TOPK = 3
MOBA_HEADS_PER_STEP = 2
PEER_HEADS = 8
PEER_NKEYS = 128
PEER_N = PEER_NKEYS * PEER_NKEYS
PEER_DKEY = 256
PEER_HALF = PEER_DKEY // 2
PEER_TOPK = 16
EPS = 1e-6
NEG = -1e30
GELU_HALF = 0.5

VMEM_LIMIT_BYTES = 48 * 1024 * 1024
PEER_VMEM_LIMIT_BYTES = 56 * 1024 * 1024
SUBLANES = 8

TM_INPROJ = 512
TM_MIX = 512
TM_PREP = 256
TM_PEER = 512
PEER_ROWS_PER_CHUNK = 4
PEER_CHUNK = PEER_ROWS_PER_CHUNK * PEER_NKEYS
PEER_CHUNKS_PER_STEP = 4
PEER_EB = PEER_CHUNKS_PER_STEP * PEER_CHUNK
BF16_ROWS = 16
PEER_LANE_TILE = 256
PEER_GATE_LANES = 256
PEER_SLAB_LANES = 128
PEER_ROWS_PER_PASS = 2
PEER_SLABS_PER_PASS = 4
TM_FINAL = 512

_CAND_PAIRS = [(r1, r2) for r1 in range(PEER_TOPK) for r2 in range(PEER_TOPK)
               if (r1 + 1) * (r2 + 1) <= PEER_TOPK]


def _params(*sem, vmem_limit_bytes=VMEM_LIMIT_BYTES):
    return pltpu.CompilerParams(dimension_semantics=sem, vmem_limit_bytes=vmem_limit_bytes)


def _rms(x, g):
    return x * lax.rsqrt(jnp.mean(x * x, axis=-1, keepdims=True) + EPS) * g


def _inproj_body(x_ref, g_ref, w_ref, b_ref, u_ref, q_ref, k_ref, v_ref):
    h = _rms(x_ref[...], g_ref[...]).astype(BF16)
    proj = jnp.dot(h, w_ref[...], preferred_element_type=F32)
    b_ref[...] = proj[:, 0:D_CONV]
    u_ref[...] = proj[:, D_CONV:2 * D_CONV] * proj[:, 2 * D_CONV:3 * D_CONV]
    o = 3 * D_CONV
    q_ref[...] = proj[:, o:o + D_ATTN].T.astype(BF16)
    k_ref[...] = proj[:, o + D_ATTN:o + 2 * D_ATTN].astype(BF16)
    v_ref[...] = proj[:, o + 2 * D_ATTN:o + 3 * D_ATTN].T.astype(BF16)


def _inproj(x2d, g, w_in_b, batch, seq):
    t = x2d.shape[0]
    tm = TM_INPROJ
    tps = seq // tm
    row = lambda i: (i, 0)
    fixed = lambda i: (0, 0)
    chan = lambda i: (i // tps, 0, i % tps)
    return pl.pallas_call(
        _inproj_body,
        grid=(t // tm,),
        in_specs=[pl.BlockSpec((tm, D_MODEL), row),
                  pl.BlockSpec((1, D_MODEL), fixed),
                  pl.BlockSpec((D_MODEL, D_PROJ), fixed)],
        out_specs=[pl.BlockSpec((tm, D_CONV), row), pl.BlockSpec((tm, D_CONV), row),
                   pl.BlockSpec((None, D_ATTN, tm), chan), pl.BlockSpec((tm, D_ATTN), row),
                   pl.BlockSpec((None, D_ATTN, tm), chan)],
        out_shape=[jax.ShapeDtypeStruct((t, D_CONV), F32), jax.ShapeDtypeStruct((t, D_CONV), F32),
                   jax.ShapeDtypeStruct((batch, D_ATTN, seq), BF16),
                   jax.ShapeDtypeStruct((t, D_ATTN), BF16),
                   jax.ShapeDtypeStruct((batch, D_ATTN, seq), BF16)],
        compiler_params=_params("parallel"),
        name="inproj",
    )(x2d, g, w_in_b)


def _moba_head(qT, k_ref, vT_ref, o_ref, rows, seq):
    nb = seq // MOBA_BLOCK
    bs = MOBA_BLOCK
    scale = float(1.0 / np.sqrt(HEAD_DIM))
    blk = lax.broadcasted_iota(jnp.int32, (nb, seq), 0)
    own = lax.shift_right_logical(lax.broadcasted_iota(jnp.int32, (nb, seq), 1),
                                  int(np.log2(bs)))
    avg = jnp.where(own == blk, 1.0 / bs, 0.0).astype(BF16)
    kmean = jnp.dot(avg, k_ref[...], preferred_element_type=F32)
    gate = jnp.dot(kmean.astype(BF16), qT, preferred_element_type=F32)
    past = blk < own
    g1 = jnp.where(past, gate, NEG)
    thr = g1
    for _ in range(MOBA_TOPK - 1):
        m = jnp.max(thr, axis=0, keepdims=True)
        thr = jnp.where(thr == m, NEG, thr)
    thr = jnp.max(thr, axis=0, keepdims=True)
    sel = jnp.where(jnp.logical_and(past, g1 >= thr), 1.0, 0.0)

    kpos = lax.broadcasted_iota(jnp.int32, (bs, bs), 0)
    qpos = lax.broadcasted_iota(jnp.int32, (bs, bs), 1)
    causal = kpos <= qpos
    for i in range(nb):
        n = (i + 1) * bs
        lanes = slice(i * bs, (i + 1) * bs)
        sT = jnp.dot(k_ref[0:n, :], qT[:, lanes], preferred_element_type=F32) * scale
        pieces = []
        for j in range(i):
            keep = sel[j:j + 1, lanes] > 0.5
            pieces.append(jnp.where(keep, sT[j * bs:(j + 1) * bs], NEG))
        pieces.append(jnp.where(causal, sT[i * bs:n], NEG))
        s = pieces[0] if i == 0 else jnp.concatenate(pieces, axis=0)
        m = jnp.max(s, axis=0, keepdims=True)
        p = jnp.exp(s - m)
        l = jnp.sum(p, axis=0, keepdims=True)
        oT = jnp.dot(vT_ref[rows, 0:n], p.astype(BF16), preferred_element_type=F32)
        o_ref[rows, lanes] = oT / l


def _moba_body(qT_ref, k_ref, vT_ref, o_ref, *, seq):
    chan = lax.broadcasted_iota(jnp.int32, (MOBA_HEADS_PER_STEP * HEAD_DIM, seq), 0)
    q2 = qT_ref[...]
    for hh in range(MOBA_HEADS_PER_STEP):
        rows = slice(hh * HEAD_DIM, (hh + 1) * HEAD_DIM)
        mine = jnp.logical_and(chan >= rows.start, chan < rows.stop)
        _moba_head(jnp.where(mine, q2, jnp.zeros((), BF16)), k_ref, vT_ref, o_ref, rows, seq)


def _moba(qT, k, vT):
    b, d, s = qT.shape
    w = MOBA_HEADS_PER_STEP * HEAD_DIM
    chan = lambda i, j: (i, j, 0)
    return pl.pallas_call(
        functools.partial(_moba_body, seq=s),
        grid=(b, d // w),
        in_specs=[pl.BlockSpec((None, w, s), chan),
                  pl.BlockSpec((s, w), lambda i, j: (i, j)),
                  pl.BlockSpec((None, w, s), chan)],
        out_specs=pl.BlockSpec((None, w, s), chan),
        out_shape=jax.ShapeDtypeStruct((b, d, s), F32),
        compiler_params=_params("parallel", "parallel"),
        name="moba",
    )(qT, k, vT)


def _mix_body(x_ref, b_ref, u_ref, uh_ref, ya_ref, cw_ref, gc_ref, ga_ref, wo_ref, gf_ref,
              x2_ref, xn_ref, uext_ref, *, tiles_per_seq):
    tm = u_ref.shape[0]
    first = (pl.program_id(0) % tiles_per_seq) == 0
    uext_ref[0:SUBLANES, :] = jnp.where(first, 0.0, uh_ref[...])
    u = u_ref[...]
    uext_ref[SUBLANES:SUBLANES + tm, :] = u
    cw = cw_ref[...]
    y = (cw[2:3, :] * u
         + cw[1:2, :] * uext_ref[SUBLANES - 1:SUBLANES - 1 + tm, :]
         + cw[0:1, :] * uext_ref[SUBLANES - 2:SUBLANES - 2 + tm, :])
    rc = _rms(b_ref[...] * y, gc_ref[...]).astype(BF16)
    ra = _rms(ya_ref[...].T, ga_ref[...]).astype(BF16)
    mix = (jnp.dot(rc, wo_ref[0:D_CONV, :], preferred_element_type=F32)
           + jnp.dot(ra, wo_ref[D_CONV:D_CONV + D_ATTN, :], preferred_element_type=F32))
    x2 = x_ref[...] + mix
    x2_ref[...] = x2
    xn_ref[...] = _rms(x2, gf_ref[...]).T.astype(BF16)


def _mix(x2d, bgate, u, yattnT, conv_w, gc, ga, w_out_b, gf, seq):
    t = x2d.shape[0]
    tm = TM_MIX
    tps = seq // tm
    row = lambda i: (i, 0)
    fixed = lambda i: (0, 0)
    halo = lambda i: (jnp.maximum(i * (tm // SUBLANES) - 1, 0), 0)
    return pl.pallas_call(
        functools.partial(_mix_body, tiles_per_seq=tps),
        grid=(t // tm,),
        in_specs=[pl.BlockSpec((tm, D_MODEL), row),
                  pl.BlockSpec((tm, D_CONV), row),
                  pl.BlockSpec((tm, D_CONV), row),
                  pl.BlockSpec((SUBLANES, D_CONV), halo),
                  pl.BlockSpec((None, D_ATTN, tm), lambda i: (i // tps, 0, i % tps)),
                  pl.BlockSpec((CONV_WIDTH, D_CONV), fixed),
                  pl.BlockSpec((1, D_CONV), fixed),
                  pl.BlockSpec((1, D_ATTN), fixed),
                  pl.BlockSpec((D_MODEL, D_MODEL), fixed),
                  pl.BlockSpec((1, D_MODEL), fixed)],
        out_specs=[pl.BlockSpec((tm, D_MODEL), row), pl.BlockSpec((D_MODEL, tm), lambda i: (0, i))],
        out_shape=[jax.ShapeDtypeStruct((t, D_MODEL), F32),
                   jax.ShapeDtypeStruct((D_MODEL, t), BF16)],
        scratch_shapes=[pltpu.VMEM((tm + SUBLANES, D_CONV), F32)],
        compiler_params=_params("parallel"),
        name="mix",
    )(x2d, bgate, u, u, yattnT, conv_w, gc, ga, w_out_b, gf)


def _oddeven_merge_sort_network(n):
    pairs = []
    p = 1
    while p < n:
        k = p
        while k >= 1:
            for j in range(k % p, n - k, 2 * k):
                for i in range(min(k, n - j - k)):
                    if (i + j) // (2 * p) == (i + j + k) // (2 * p):
                        pairs.append((i + j, i + j + k))
            k //= 2
        p *= 2
    return pairs


_SORT_NET = _oddeven_merge_sort_network(PEER_TOPK)


def _exchange(g, i, j):
    g[i], g[j] = jnp.maximum(g[i], g[j]), jnp.minimum(g[i], g[j])


def _slabs(s):
    return [s[SUBLANES * i:SUBLANES * (i + 1), :] for i in range(s.shape[0] // SUBLANES)]


def _sorted_top(slabs):
    n = PEER_TOPK
    g = list(slabs)
    assert len(g) == n
    for i, j in _SORT_NET:
        _exchange(g, i, j)
    shift = SUBLANES // 2
    while shift >= 1:
        rolled = [pltpu.roll(x, shift, axis=0) for x in g]
        g = [jnp.maximum(g[i], rolled[n - 1 - i]) for i in range(n)]
        d = n // 2
        while d >= 1:
            for i in range(n):
                if i & d == 0:
                    _exchange(g, i, i + d)
            d //= 2
        shift //= 2
    return g


def _kth_largest_packed(cands, k):
    sub = lax.broadcasted_iota(jnp.int32, cands[0].shape, 0)
    packed = []
    for q in range(0, len(cands), SUBLANES):
        group = cands[q:q + SUBLANES]
        x = group[0] if len(group) == SUBLANES else jnp.full_like(cands[0], NEG)
        for p, cnd in enumerate(group):
            if p or len(group) < SUBLANES:
                x = jnp.where(sub == p, cnd, x)
        packed.append(x)
    out = []
    for _ in range(k):
        m = functools.reduce(jnp.maximum, packed)
        m = jnp.max(m, axis=0, keepdims=True)
        out.append(jnp.broadcast_to(m, cands[0].shape))
        packed = [jnp.where(x == m, NEG, x) for x in packed]
    return out


def _prep_head(s1, s2):
    last = PEER_TOPK - 1
    g1, g2 = _slabs(s1), _slabs(s2)
    v1, v2 = _sorted_top(g1), _sorted_top(g2)
    cands = [v1[r1] + v2[r2] for r1, r2 in _CAND_PAIRS]
    tops = _kth_largest_packed(cands, PEER_TOPK)
    tau = tops[last]
    z = functools.reduce(jnp.add, [jnp.exp(t - tops[0]) for t in tops])
    cs = [jnp.zeros_like(tau) for _ in range(PEER_TOPK)]
    for cnd, (r1, _) in zip(cands, _CAND_PAIRS):
        cs[r1] = cs[r1] + jnp.where(cnd >= tau, 1.0, 0.0)
    scale = GELU_HALF / z
    rank2, cnt, e1, e2 = [], [], [], []
    for a, b in zip(g1, g2):
        c = jnp.zeros_like(a)
        for r in range(PEER_TOPK):
            c = jnp.where(a == v1[r], cs[r], c)
        cnt.append(c)
        rk = jnp.full_like(b, float(PEER_TOPK))
        for r in range(last, -1, -1):
            rk = jnp.where(b >= v2[r], float(r), rk)
        rank2.append(rk)
        e1.append(jnp.where(a >= v1[last], jnp.exp(a - v1[0]), 0.0) * scale)
        e2.append(jnp.where(b >= v2[last], jnp.exp(b - v2[0]), 0.0))
    cat = lambda parts: jnp.concatenate(parts, axis=0)
    return cat(rank2), cat(cnt), cat(e1), cat(e2)


def _prep_body(xnT_ref, wqT_ref, k1_ref, k2_ref, rk_ref, cn_ref, e1_ref, e2_ref, qry_ref):
    qry_ref[...] = jnp.dot(wqT_ref[...], xnT_ref[...], preferred_element_type=F32)
    for h in range(PEER_HEADS):
        o = h * PEER_DKEY
        q1 = qry_ref[o:o + PEER_HALF, :].astype(BF16)
        q2 = qry_ref[o + PEER_HALF:o + PEER_DKEY, :].astype(BF16)
        s1 = jnp.dot(k1_ref[h], q1, preferred_element_type=F32)
        s2 = jnp.dot(k2_ref[h], q2, preferred_element_type=F32)
        rank2, cnt, e1, e2 = _prep_head(s1, s2)
        rk_ref[h] = rank2.astype(BF16)
        e2_ref[h] = e2.astype(BF16)
        for lt in range(s1.shape[1] // PEER_SLAB_LANES):
            lanes = slice(lt * PEER_SLAB_LANES, (lt + 1) * PEER_SLAB_LANES)
            cn_ref[h, lt] = cnt[:, lanes]
            e1_ref[h, lt] = e1[:, lanes]


def _prep(xnT, wqT_b, k1_b, k2_b):
    t = xnT.shape[1]
    tm = TM_PREP
    col = lambda i: (0, i)
    col3 = lambda i: (0, 0, i)
    slab = lambda i: (0, i, 0, 0)
    hk = (PEER_HEADS, PEER_NKEYS, tm)
    hk_slab = (PEER_HEADS, tm // PEER_SLAB_LANES, PEER_NKEYS, PEER_SLAB_LANES)
    slab_shape = jax.ShapeDtypeStruct((PEER_HEADS, t // PEER_SLAB_LANES, PEER_NKEYS, PEER_SLAB_LANES), F32)
    return pl.pallas_call(
        _prep_body,
        grid=(t // tm,),
        in_specs=[pl.BlockSpec((D_MODEL, tm), col),
                  pl.BlockSpec((PEER_HEADS * PEER_DKEY, D_MODEL), lambda i: (0, 0)),
                  pl.BlockSpec((PEER_HEADS, PEER_NKEYS, PEER_HALF), lambda i: (0, 0, 0)),
                  pl.BlockSpec((PEER_HEADS, PEER_NKEYS, PEER_HALF), lambda i: (0, 0, 0))],
        out_specs=[pl.BlockSpec(hk, col3), pl.BlockSpec(hk_slab, slab), pl.BlockSpec(hk_slab, slab),
                   pl.BlockSpec(hk, lambda i: (0, 0, jnp.bitwise_xor(i, 1)))],
        out_shape=[jax.ShapeDtypeStruct((PEER_HEADS, PEER_NKEYS, t), BF16), slab_shape, slab_shape,
                   jax.ShapeDtypeStruct((PEER_HEADS, PEER_NKEYS, t), BF16)],
        scratch_shapes=[pltpu.VMEM((PEER_HEADS * PEER_DKEY, tm), F32)],
        compiler_params=_params("parallel"),
        name="prep",
    )(xnT, wqT_b, k1_b, k2_b)


def _peer_hidden(c, xnT_ref, dn_ref, h_ref):
    rows = slice(c * PEER_CHUNK, (c + 1) * PEER_CHUNK)
    h_ref[c] = jnp.dot(dn_ref[rows, :], xnT_ref[...], preferred_element_type=F32)


def _swapped_lanes(lt):
    assert TM_PREP == PEER_GATE_LANES and (TM_PEER // PEER_GATE_LANES) % 2 == 0
    return slice((lt ^ 1) * PEER_GATE_LANES, ((lt ^ 1) + 1) * PEER_GATE_LANES)


def _row_tile(ref, h, lt, i1):
    per = PEER_GATE_LANES // PEER_SLAB_LANES
    parts = [ref[h, lt * per + k, pl.ds(i1, SUBLANES, stride=0), :] for k in range(per)]
    x = jnp.concatenate(parts, axis=1)
    return jnp.concatenate([x] * (BF16_ROWS // SUBLANES), axis=0).astype(BF16)


def _peer_gate_chunk(blk, c, h_ref, rk_ref, cn_ref, e1_ref, e2_ref, a_ref):
    tm = a_ref.shape[1]
    sqrt_half = float(np.sqrt(0.5))
    slabs = PEER_NKEYS // BF16_ROWS
    zero = jnp.zeros((), BF16)
    row0 = (blk * PEER_CHUNKS_PER_STEP + c) * PEER_ROWS_PER_CHUNK
    for g in range(0, PEER_ROWS_PER_CHUNK, PEER_ROWS_PER_PASS):
        rows = range(g, g + PEER_ROWS_PER_PASS)
        act = {}
        for r in rows:
            hr = h_ref[c, r * PEER_NKEYS:(r + 1) * PEER_NKEYS, :]
            act[r] = (hr * (1.0 + lax.erf(hr * sqrt_half))).astype(BF16)
        for lt in range(tm // PEER_GATE_LANES):
            lanes = slice(lt * PEER_GATE_LANES, (lt + 1) * PEER_GATE_LANES)
            for s0 in range(0, slabs, PEER_SLABS_PER_PASS):
                w = {(r, s): None for r in rows for s in range(s0, s0 + PEER_SLABS_PER_PASS)}
                for h in range(PEER_HEADS):
                    c16 = {r: _row_tile(cn_ref, h, lt, row0 + r) for r in rows}
                    e16 = {r: _row_tile(e1_ref, h, lt, row0 + r) for r in rows}
                    for s in range(s0, s0 + PEER_SLABS_PER_PASS):
                        sl = slice(s * BF16_ROWS, (s + 1) * BF16_ROWS)
                        rk = rk_ref[h, sl, lanes]
                        e2 = e2_ref[h, sl, _swapped_lanes(lt)]
                        for r in rows:
                            term = jnp.where(rk < c16[r], e2 * e16[r], zero)
                            w[r, s] = term if w[r, s] is None else w[r, s] + term
                for (r, s), ws in w.items():
                    sl = slice(s * BF16_ROWS, (s + 1) * BF16_ROWS)
                    base = c * PEER_CHUNK + r * PEER_NKEYS
                    a_ref[base + s * BF16_ROWS:base + (s + 1) * BF16_ROWS, lanes] = ws * act[r][sl, lanes]


def _peer_up_piece(p, upT_ref, a_ref, o_ref):
    lane_tiles = o_ref.shape[1] // PEER_LANE_TILE
    band = o_ref.shape[0] // (PEER_CHUNKS_PER_STEP // lane_tiles)
    rows = slice((p // lane_tiles) * band, (p // lane_tiles + 1) * band)
    lanes = slice((p % lane_tiles) * PEER_LANE_TILE, (p % lane_tiles + 1) * PEER_LANE_TILE)
    o_ref[rows, lanes] += jnp.dot(upT_ref[rows, :], a_ref[:, lanes], preferred_element_type=F32)


def _peer_step(blk, xnT_ref, dn_ref, upT_ref, rk_ref, cn_ref, e1_ref, e2_ref, o_ref, h_ref,
               a_out, a_in):
    nch = PEER_CHUNKS_PER_STEP
    assert nch % (o_ref.shape[1] // PEER_LANE_TILE) == 0
    if a_out is not None:
        _peer_hidden(0, xnT_ref, dn_ref, h_ref)
    for c in range(nch):
        if a_out is not None:
            if c + 1 < nch:
                _peer_hidden(c + 1, xnT_ref, dn_ref, h_ref)
            _peer_gate_chunk(blk, c, h_ref, rk_ref, cn_ref, e1_ref, e2_ref, a_out)
        if a_in is not None:
            _peer_up_piece(c, upT_ref, a_in, o_ref)


def _peer_body(xnT_ref, dn_ref, upT_ref, rk_ref, cn_ref, e1_ref, e2_ref, o_ref, a0_ref, a1_ref,
               h_ref):
    j = pl.program_id(1)
    nblk = PEER_N // PEER_EB
    step = functools.partial(_peer_step, j, xnT_ref, dn_ref, upT_ref, rk_ref, cn_ref, e1_ref,
                             e2_ref, o_ref, h_ref)
    even = lax.rem(j, 2) == 0
    inner = jnp.logical_and(j > 0, j < nblk)

    @pl.when(j == 0)
    def _():
        o_ref[...] = jnp.zeros_like(o_ref)
        step(a0_ref, None)

    @pl.when(jnp.logical_and(inner, even))
    def _():
        step(a0_ref, a1_ref)

    @pl.when(jnp.logical_and(inner, jnp.logical_not(even)))
    def _():
        step(a1_ref, a0_ref)

    @pl.when(j == nblk)
    def _():
        step(None, a1_ref if (nblk - 1) % 2 else a0_ref)


def _peer(xnT, down_b, upT_b, rk, cn, e1, e2):
    t = xnT.shape[1]
    tm = TM_PEER
    nblk = PEER_N // PEER_EB
    hk = (PEER_HEADS, PEER_NKEYS, tm)
    hk_slab = (PEER_HEADS, tm // PEER_SLAB_LANES, PEER_NKEYS, PEER_SLAB_LANES)
    tok3 = lambda i, j: (0, 0, i)
    slab = lambda i, j: (0, i, 0, 0)
    return pl.pallas_call(
        _peer_body,
        grid=(t // tm, nblk + 1),
        in_specs=[pl.BlockSpec((D_MODEL, tm), lambda i, j: (0, i)),
                  pl.BlockSpec((PEER_EB, D_MODEL), lambda i, j: (jnp.minimum(j, nblk - 1), 0)),
                  pl.BlockSpec((D_MODEL, PEER_EB), lambda i, j: (0, jnp.maximum(j - 1, 0))),
                  pl.BlockSpec(hk, tok3), pl.BlockSpec(hk_slab, slab),
                  pl.BlockSpec(hk_slab, slab), pl.BlockSpec(hk, tok3)],
        out_specs=pl.BlockSpec((D_MODEL, tm), lambda i, j: (0, i)),
        out_shape=jax.ShapeDtypeStruct((D_MODEL, t), F32),
        scratch_shapes=[pltpu.VMEM((PEER_EB, tm), BF16), pltpu.VMEM((PEER_EB, tm), BF16),
                        pltpu.VMEM((PEER_CHUNKS_PER_STEP, PEER_CHUNK, tm), F32)],
        compiler_params=_params("parallel", "arbitrary", vmem_limit_bytes=PEER_VMEM_LIMIT_BYTES),
        name="peer",
    )(xnT, down_b, upT_b, rk, cn, e1, e2)


def _final_body(x2_ref, pT_ref, g_ref, o_ref):
    o_ref[...] = _rms(x2_ref[...] + pT_ref[...].T, g_ref[...])


def _final(x2, peerT, g):
    t = x2.shape[0]
    tm = TM_FINAL
    row = lambda i: (i, 0)
    return pl.pallas_call(
        _final_body,
        grid=(t // tm,),
        in_specs=[pl.BlockSpec((tm, D_MODEL), row), pl.BlockSpec((D_MODEL, tm), lambda i: (0, i)),
                  pl.BlockSpec((1, D_MODEL), lambda i: (0, 0))],
        out_specs=pl.BlockSpec((tm, D_MODEL), row),
        out_shape=jax.ShapeDtypeStruct((t, D_MODEL), F32),
        compiler_params=_params("parallel"),
        name="final",
    )(x2, peerT, g)


def kernel(x, norm_mix_g, w_in, conv_w, norm_conv_out_g, norm_attn_out_g, w_out, norm_ffn_g,
           peer_w_query, peer_sub_keys1, peer_sub_keys2, peer_down, peer_up, final_norm_g):
    batch, seq, d = x.shape
    assert w_in.shape[0] == 1, "one layer: the residual stream between layers is not implemented"
    assert d == D_MODEL and seq % MOBA_BLOCK == 0 and seq % TM_MIX == 0 and seq % TM_INPROJ == 0
    assert seq // MOBA_BLOCK > MOBA_TOPK
    x2d = x.reshape(batch * seq, d)
    bgate, u, qT, k, vT = _inproj(x2d, norm_mix_g[0].reshape(1, D_MODEL), w_in[0].astype(BF16),
                                  batch, seq)
    yT = _moba(qT, k, vT)
    x2, xnT = _mix(x2d, bgate, u, yT, conv_w[0], norm_conv_out_g[0].reshape(1, D_CONV),
                   norm_attn_out_g[0].reshape(1, D_ATTN), w_out[0].astype(BF16),
                   norm_ffn_g[0].reshape(1, D_MODEL), seq)
    rk, cn, e1, e2 = _prep(xnT, peer_w_query[0].T.astype(BF16),
                           peer_sub_keys1[0].astype(BF16), peer_sub_keys2[0].astype(BF16))
    outT = _peer(xnT, peer_down[0].astype(BF16), peer_up[0].T.astype(BF16), rk, cn, e1, e2)
    out = _final(x2, outT, final_norm_g.reshape(1, D_MODEL))
    return out.reshape(batch, seq, d)
```

```python
import functools

import numpy as np
import jax
import jax.numpy as jnp
from jax import lax
from jax.experimental import pallas as pl
from jax.experimental.pallas import tpu as pltpu

F32 = jnp.float32
BF16 = jnp.bfloat16

D_MODEL = 1024
D_CONV = 512
D_ATTN = 512
D_PROJ = 3 * D_CONV + 3 * D_ATTN
CONV_WIDTH = 3
ATTN_HEADS = 8
HEAD_DIM = 64
MOBA_BLOCK = 256
MOBA_TOPK = 3
MOBA_HEADS_PER_STEP = 2
PEER_HEADS = 8
PEER_NKEYS = 128
PEER_N = PEER_NKEYS * PEER_NKEYS
PEER_DKEY = 256
PEER_HALF = PEER_DKEY // 2
PEER_TOPK = 16
EPS = 1e-6
NEG = -1e30
GELU_HALF = 0.5

VMEM_LIMIT_BYTES = 48 * 1024 * 1024
PEER_VMEM_LIMIT_BYTES = 56 * 1024 * 1024
SUBLANES = 8

TM_INPROJ = 512
TM_MIX = 512
TM_PREP = 256
TM_PEER = 512
PEER_ROWS_PER_CHUNK = 1
PEER_CHUNK = PEER_ROWS_PER_CHUNK * PEER_NKEYS
PEER_CHUNKS_PER_STEP = 16
PEER_EB = PEER_CHUNKS_PER_STEP * PEER_CHUNK
BF16_ROWS = 16
PEER_LANE_TILE = 256
PEER_GATE_LANES = 256
PEER_SLAB_LANES = 128
PEER_ROWS_PER_PASS = 1
PEER_SLABS_PER_PASS = 8
PEER_UP_PIECES = 4
PEER_HIDDEN_CHUNKS = 1
PEER_HIDDEN_LANES = 512
PEER_UP_PHASE = 0
TM_FINAL = 512

_CAND_PAIRS = [(r1, r2) for r1 in range(PEER_TOPK) for r2 in range(PEER_TOPK)
               if (r1 + 1) * (r2 + 1) <= PEER_TOPK]


def _params(*sem, vmem_limit_bytes=VMEM_LIMIT_BYTES):
    return pltpu.CompilerParams(dimension_semantics=sem, vmem_limit_bytes=vmem_limit_bytes)


def _rms(x, g):
    return x * lax.rsqrt(jnp.mean(x * x, axis=-1, keepdims=True) + EPS) * g


def _inproj_body(x_ref, g_ref, w_ref, b_ref, u_ref, q_ref, k_ref, v_ref):
    h = _rms(x_ref[...], g_ref[...]).astype(BF16)
    proj = jnp.dot(h, w_ref[...], preferred_element_type=F32)
    b_ref[...] = proj[:, 0:D_CONV]
    u_ref[...] = proj[:, D_CONV:2 * D_CONV] * proj[:, 2 * D_CONV:3 * D_CONV]
    o = 3 * D_CONV
    q_ref[...] = proj[:, o:o + D_ATTN].T.astype(BF16)
    k_ref[...] = proj[:, o + D_ATTN:o + 2 * D_ATTN].astype(BF16)
    v_ref[...] = proj[:, o + 2 * D_ATTN:o + 3 * D_ATTN].T.astype(BF16)


def _inproj(x2d, g, w_in_b, batch, seq):
    t = x2d.shape[0]
    tm = TM_INPROJ
    tps = seq // tm
    row = lambda i: (i, 0)
    fixed = lambda i: (0, 0)
    chan = lambda i: (i // tps, 0, i % tps)
    return pl.pallas_call(
        _inproj_body,
        grid=(t // tm,),
        in_specs=[pl.BlockSpec((tm, D_MODEL), row),
                  pl.BlockSpec((1, D_MODEL), fixed),
                  pl.BlockSpec((D_MODEL, D_PROJ), fixed)],
        out_specs=[pl.BlockSpec((tm, D_CONV), row), pl.BlockSpec((tm, D_CONV), row),
                   pl.BlockSpec((None, D_ATTN, tm), chan), pl.BlockSpec((tm, D_ATTN), row),
                   pl.BlockSpec((None, D_ATTN, tm), chan)],
        out_shape=[jax.ShapeDtypeStruct((t, D_CONV), F32), jax.ShapeDtypeStruct((t, D_CONV), F32),
                   jax.ShapeDtypeStruct((batch, D_ATTN, seq), BF16),
                   jax.ShapeDtypeStruct((t, D_ATTN), BF16),
                   jax.ShapeDtypeStruct((batch, D_ATTN, seq), BF16)],
        compiler_params=_params("parallel"),
        name="inproj",
    )(x2d, g, w_in_b)


def _moba_head(qT, k_ref, vT_ref, o_ref, rows, seq):
    nb = seq // MOBA_BLOCK
    bs = MOBA_BLOCK
    scale = float(1.0 / np.sqrt(HEAD_DIM))
    blk = lax.broadcasted_iota(jnp.int32, (nb, seq), 0)
    own = lax.shift_right_logical(lax.broadcasted_iota(jnp.int32, (nb, seq), 1),
                                  int(np.log2(bs)))
    avg = jnp.where(own == blk, 1.0 / bs, 0.0).astype(BF16)
    kmean = jnp.dot(avg, k_ref[...], preferred_element_type=F32)
    gate = jnp.dot(kmean.astype(BF16), qT, preferred_element_type=F32)
    past = blk < own
    g1 = jnp.where(past, gate, NEG)
    thr = g1
    for _ in range(MOBA_TOPK - 1):
        m = jnp.max(thr, axis=0, keepdims=True)
        thr = jnp.where(thr == m, NEG, thr)
    thr = jnp.max(thr, axis=0, keepdims=True)
    sel = jnp.where(jnp.logical_and(past, g1 >= thr), 1.0, 0.0)

    kpos = lax.broadcasted_iota(jnp.int32, (bs, bs), 0)
    qpos = lax.broadcasted_iota(jnp.int32, (bs, bs), 1)
    causal = kpos <= qpos
    for i in range(nb):
        n = (i + 1) * bs
        lanes = slice(i * bs, (i + 1) * bs)
        sT = jnp.dot(k_ref[0:n, :], qT[:, lanes], preferred_element_type=F32) * scale
        pieces = []
        for j in range(i):
            keep = sel[j:j + 1, lanes] > 0.5
            pieces.append(jnp.where(keep, sT[j * bs:(j + 1) * bs], NEG))
        pieces.append(jnp.where(causal, sT[i * bs:n], NEG))
        s = pieces[0] if i == 0 else jnp.concatenate(pieces, axis=0)
        m = jnp.max(s, axis=0, keepdims=True)
        p = jnp.exp(s - m)
        l = jnp.sum(p, axis=0, keepdims=True)
        oT = jnp.dot(vT_ref[rows, 0:n], p.astype(BF16), preferred_element_type=F32)
        o_ref[rows, lanes] = oT / l


def _moba_body(qT_ref, k_ref, vT_ref, o_ref, *, seq):
    chan = lax.broadcasted_iota(jnp.int32, (MOBA_HEADS_PER_STEP * HEAD_DIM, seq), 0)
    q2 = qT_ref[...]
    for hh in range(MOBA_HEADS_PER_STEP):
        rows = slice(hh * HEAD_DIM, (hh + 1) * HEAD_DIM)
        mine = jnp.logical_and(chan >= rows.start, chan < rows.stop)
        _moba_head(jnp.where(mine, q2, jnp.zeros((), BF16)), k_ref, vT_ref, o_ref, rows, seq)


def _moba(qT, k, vT):
    b, d, s = qT.shape
    w = MOBA_HEADS_PER_STEP * HEAD_DIM
    chan = lambda i, j: (i, j, 0)
    return pl.pallas_call(
        functools.partial(_moba_body, seq=s),
        grid=(b, d // w),
        in_specs=[pl.BlockSpec((None, w, s), chan),
                  pl.BlockSpec((s, w), lambda i, j: (i, j)),
                  pl.BlockSpec((None, w, s), chan)],
        out_specs=pl.BlockSpec((None, w, s), chan),
        out_shape=jax.ShapeDtypeStruct((b, d, s), F32),
        compiler_params=_params("parallel", "parallel"),
        name="moba",
    )(qT, k, vT)


def _mix_body(x_ref, b_ref, u_ref, uh_ref, ya_ref, cw_ref, gc_ref, ga_ref, wo_ref, gf_ref,
              x2_ref, xn_ref, uext_ref, *, tiles_per_seq):
    tm = u_ref.shape[0]
    first = (pl.program_id(0) % tiles_per_seq) == 0
    uext_ref[0:SUBLANES, :] = jnp.where(first, 0.0, uh_ref[...])
    u = u_ref[...]
    uext_ref[SUBLANES:SUBLANES + tm, :] = u
    cw = cw_ref[...]
    y = (cw[2:3, :] * u
         + cw[1:2, :] * uext_ref[SUBLANES - 1:SUBLANES - 1 + tm, :]
         + cw[0:1, :] * uext_ref[SUBLANES - 2:SUBLANES - 2 + tm, :])
    rc = _rms(b_ref[...] * y, gc_ref[...]).astype(BF16)
    ra = _rms(ya_ref[...].T, ga_ref[...]).astype(BF16)
    mix = (jnp.dot(rc, wo_ref[0:D_CONV, :], preferred_element_type=F32)
           + jnp.dot(ra, wo_ref[D_CONV:D_CONV + D_ATTN, :], preferred_element_type=F32))
    x2 = x_ref[...] + mix
    x2_ref[...] = x2
    xn_ref[...] = _rms(x2, gf_ref[...]).T.astype(BF16)


def _mix(x2d, bgate, u, yattnT, conv_w, gc, ga, w_out_b, gf, seq):
    t = x2d.shape[0]
    tm = TM_MIX
    tps = seq // tm
    row = lambda i: (i, 0)
    fixed = lambda i: (0, 0)
    halo = lambda i: (jnp.maximum(i * (tm // SUBLANES) - 1, 0), 0)
    return pl.pallas_call(
        functools.partial(_mix_body, tiles_per_seq=tps),
        grid=(t // tm,),
        in_specs=[pl.BlockSpec((tm, D_MODEL), row),
                  pl.BlockSpec((tm, D_CONV), row),
                  pl.BlockSpec((tm, D_CONV), row),
                  pl.BlockSpec((SUBLANES, D_CONV), halo),
                  pl.BlockSpec((None, D_ATTN, tm), lambda i: (i // tps, 0, i % tps)),
                  pl.BlockSpec((CONV_WIDTH, D_CONV), fixed),
                  pl.BlockSpec((1, D_CONV), fixed),
                  pl.BlockSpec((1, D_ATTN), fixed),
                  pl.BlockSpec((D_MODEL, D_MODEL), fixed),
                  pl.BlockSpec((1, D_MODEL), fixed)],
        out_specs=[pl.BlockSpec((tm, D_MODEL), row), pl.BlockSpec((D_MODEL, tm), lambda i: (0, i))],
        out_shape=[jax.ShapeDtypeStruct((t, D_MODEL), F32),
                   jax.ShapeDtypeStruct((D_MODEL, t), BF16)],
        scratch_shapes=[pltpu.VMEM((tm + SUBLANES, D_CONV), F32)],
        compiler_params=_params("parallel"),
        name="mix",
    )(x2d, bgate, u, u, yattnT, conv_w, gc, ga, w_out_b, gf)


def _oddeven_merge_sort_network(n):
    pairs = []
    p = 1
    while p < n:
        k = p
        while k >= 1:
            for j in range(k % p, n - k, 2 * k):
                for i in range(min(k, n - j - k)):
                    if (i + j) // (2 * p) == (i + j + k) // (2 * p):
                        pairs.append((i + j, i + j + k))
            k //= 2
        p *= 2
    return pairs


_SORT_NET = _oddeven_merge_sort_network(PEER_TOPK)


def _exchange(g, i, j):
    g[i], g[j] = jnp.maximum(g[i], g[j]), jnp.minimum(g[i], g[j])


def _slabs(s):
    return [s[SUBLANES * i:SUBLANES * (i + 1), :] for i in range(s.shape[0] // SUBLANES)]


def _sorted_top(slabs):
    n = PEER_TOPK
    g = list(slabs)
    assert len(g) == n
    for i, j in _SORT_NET:
        _exchange(g, i, j)
    shift = SUBLANES // 2
    while shift >= 1:
        rolled = [pltpu.roll(x, shift, axis=0) for x in g]
        g = [jnp.maximum(g[i], rolled[n - 1 - i]) for i in range(n)]
        d = n // 2
        while d >= 1:
            for i in range(n):
                if i & d == 0:
                    _exchange(g, i, i + d)
            d //= 2
        shift //= 2
    return g


def _kth_largest_packed(cands, k):
    sub = lax.broadcasted_iota(jnp.int32, cands[0].shape, 0)
    packed = []
    for q in range(0, len(cands), SUBLANES):
        group = cands[q:q + SUBLANES]
        x = group[0] if len(group) == SUBLANES else jnp.full_like(cands[0], NEG)
        for p, cnd in enumerate(group):
            if p or len(group) < SUBLANES:
                x = jnp.where(sub == p, cnd, x)
        packed.append(x)
    out = []
    for _ in range(k):
        m = functools.reduce(jnp.maximum, packed)
        m = jnp.max(m, axis=0, keepdims=True)
        out.append(jnp.broadcast_to(m, cands[0].shape))
        packed = [jnp.where(x == m, NEG, x) for x in packed]
    return out


def _prep_head(s1, s2):
    last = PEER_TOPK - 1
    g1, g2 = _slabs(s1), _slabs(s2)
    v1, v2 = _sorted_top(g1), _sorted_top(g2)
    cands = [v1[r1] + v2[r2] for r1, r2 in _CAND_PAIRS]
    tops = _kth_largest_packed(cands, PEER_TOPK)
    tau = tops[last]
    z = functools.reduce(jnp.add, [jnp.exp(t - tops[0]) for t in tops])
    cs = [jnp.zeros_like(tau) for _ in range(PEER_TOPK)]
    for cnd, (r1, _) in zip(cands, _CAND_PAIRS):
        cs[r1] = cs[r1] + jnp.where(cnd >= tau, 1.0, 0.0)
    scale = GELU_HALF / z
    rank2, cnt, e1, e2 = [], [], [], []
    for a, b in zip(g1, g2):
        c = jnp.zeros_like(a)
        for r in range(PEER_TOPK):
            c = jnp.where(a == v1[r], cs[r], c)
        cnt.append(c)
        rk = jnp.full_like(b, float(PEER_TOPK))
        for r in range(last, -1, -1):
            rk = jnp.where(b >= v2[r], float(r), rk)
        rank2.append(rk)
        e1.append(jnp.where(a >= v1[last], jnp.exp(a - v1[0]), 0.0) * scale)
        e2.append(jnp.where(b >= v2[last], jnp.exp(b - v2[0]), 0.0))
    cat = lambda parts: jnp.concatenate(parts, axis=0)
    return cat(rank2), cat(cnt), cat(e1), cat(e2)


def _prep_body(xnT_ref, wqT_ref, k1_ref, k2_ref, rk_ref, cn_ref, e1_ref, e2_ref, qry_ref):
    qry_ref[...] = jnp.dot(wqT_ref[...], xnT_ref[...], preferred_element_type=F32)
    for h in range(PEER_HEADS):
        o = h * PEER_DKEY
        q1 = qry_ref[o:o + PEER_HALF, :].astype(BF16)
        q2 = qry_ref[o + PEER_HALF:o + PEER_DKEY, :].astype(BF16)
        s1 = jnp.dot(k1_ref[h], q1, preferred_element_type=F32)
        s2 = jnp.dot(k2_ref[h], q2, preferred_element_type=F32)
        rank2, cnt, e1, e2 = _prep_head(s1, s2)
        rk_ref[h] = rank2.astype(BF16)
        e2_ref[h] = e2.astype(BF16)
        for lt in range(s1.shape[1] // PEER_SLAB_LANES):
            lanes = slice(lt * PEER_SLAB_LANES, (lt + 1) * PEER_SLAB_LANES)
            cn_ref[h, lt] = cnt[:, lanes]
            e1_ref[h, lt] = e1[:, lanes]


def _prep(xnT, wqT_b, k1_b, k2_b):
    t = xnT.shape[1]
    tm = TM_PREP
    col = lambda i: (0, i)
    col3 = lambda i: (0, 0, i)
    slab = lambda i: (0, i, 0, 0)
    hk = (PEER_HEADS, PEER_NKEYS, tm)
    hk_slab = (PEER_HEADS, tm // PEER_SLAB_LANES, PEER_NKEYS, PEER_SLAB_LANES)
    slab_shape = jax.ShapeDtypeStruct((PEER_HEADS, t // PEER_SLAB_LANES, PEER_NKEYS, PEER_SLAB_LANES), F32)
    return pl.pallas_call(
        _prep_body,
        grid=(t // tm,),
        in_specs=[pl.BlockSpec((D_MODEL, tm), col),
                  pl.BlockSpec((PEER_HEADS * PEER_DKEY, D_MODEL), lambda i: (0, 0)),
                  pl.BlockSpec((PEER_HEADS, PEER_NKEYS, PEER_HALF), lambda i: (0, 0, 0)),
                  pl.BlockSpec((PEER_HEADS, PEER_NKEYS, PEER_HALF), lambda i: (0, 0, 0))],
        out_specs=[pl.BlockSpec(hk, col3), pl.BlockSpec(hk_slab, slab), pl.BlockSpec(hk_slab, slab),
                   pl.BlockSpec(hk, lambda i: (0, 0, jnp.bitwise_xor(i, 1)))],
        out_shape=[jax.ShapeDtypeStruct((PEER_HEADS, PEER_NKEYS, t), BF16), slab_shape, slab_shape,
                   jax.ShapeDtypeStruct((PEER_HEADS, PEER_NKEYS, t), BF16)],
        scratch_shapes=[pltpu.VMEM((PEER_HEADS * PEER_DKEY, tm), F32)],
        compiler_params=_params("parallel"),
        name="prep",
    )(xnT, wqT_b, k1_b, k2_b)


def _peer_hidden(k, xnT_ref, dn_ref, h_ref):
    n = PEER_HIDDEN_CHUNKS * PEER_CHUNK
    rows = slice(k * n, (k + 1) * n)
    for lt in range(h_ref.shape[1] // PEER_HIDDEN_LANES):
        lanes = slice(lt * PEER_HIDDEN_LANES, (lt + 1) * PEER_HIDDEN_LANES)
        h_ref[rows, lanes] = jnp.dot(dn_ref[rows, :], xnT_ref[:, lanes], preferred_element_type=F32)


def _swapped_lanes(lt):
    assert TM_PREP == PEER_GATE_LANES and (TM_PEER // PEER_GATE_LANES) % 2 == 0
    return slice((lt ^ 1) * PEER_GATE_LANES, ((lt ^ 1) + 1) * PEER_GATE_LANES)


def _row_tile(ref, h, lt, i1):
    per = PEER_GATE_LANES // PEER_SLAB_LANES
    parts = [ref[h, lt * per + k, pl.ds(i1, SUBLANES, stride=0), :] for k in range(per)]
    x = jnp.concatenate(parts, axis=1)
    return jnp.concatenate([x] * (BF16_ROWS // SUBLANES), axis=0).astype(BF16)


def _peer_gate_chunk(blk, c, h_ref, rk_ref, cn_ref, e1_ref, e2_ref, a_ref):
    tm = a_ref.shape[1]
    sqrt_half = float(np.sqrt(0.5))
    slabs = PEER_NKEYS // BF16_ROWS
    zero = jnp.zeros((), BF16)
    row0 = (blk * PEER_CHUNKS_PER_STEP + c) * PEER_ROWS_PER_CHUNK
    for g in range(0, PEER_ROWS_PER_CHUNK, PEER_ROWS_PER_PASS):
        rows = range(g, g + PEER_ROWS_PER_PASS)
        act = {}
        for r in rows:
            hr = h_ref[c * PEER_CHUNK + r * PEER_NKEYS:c * PEER_CHUNK + (r + 1) * PEER_NKEYS, :]
            act[r] = (hr * (1.0 + lax.erf(hr * sqrt_half))).astype(BF16)
        for lt in range(tm // PEER_GATE_LANES):
            lanes = slice(lt * PEER_GATE_LANES, (lt + 1) * PEER_GATE_LANES)
            for s0 in range(0, slabs, PEER_SLABS_PER_PASS):
                w = {(r, s): None for r in rows for s in range(s0, s0 + PEER_SLABS_PER_PASS)}
                for h in range(PEER_HEADS):
                    c16 = {r: _row_tile(cn_ref, h, lt, row0 + r) for r in rows}
                    e16 = {r: _row_tile(e1_ref, h, lt, row0 + r) for r in rows}
                    for s in range(s0, s0 + PEER_SLABS_PER_PASS):
                        sl = slice(s * BF16_ROWS, (s + 1) * BF16_ROWS)
                        rk = rk_ref[h, sl, lanes]
                        e2 = e2_ref[h, sl, _swapped_lanes(lt)]
                        for r in rows:
                            term = jnp.where(rk < c16[r], e2 * e16[r], zero)
                            w[r, s] = term if w[r, s] is None else w[r, s] + term
                for (r, s), ws in w.items():
                    sl = slice(s * BF16_ROWS, (s + 1) * BF16_ROWS)
                    base = c * PEER_CHUNK + r * PEER_NKEYS
                    a_ref[base + s * BF16_ROWS:base + (s + 1) * BF16_ROWS, lanes] = ws * act[r][sl, lanes]


def _peer_up_piece(p, upT_ref, a_ref, o_ref):
    lane_tiles = o_ref.shape[1] // PEER_LANE_TILE
    band = o_ref.shape[0] // (PEER_UP_PIECES // lane_tiles)
    rows = slice((p // lane_tiles) * band, (p // lane_tiles + 1) * band)
    lanes = slice((p % lane_tiles) * PEER_LANE_TILE, (p % lane_tiles + 1) * PEER_LANE_TILE)
    o_ref[rows, lanes] += jnp.dot(upT_ref[rows, :], a_ref[:, lanes], preferred_element_type=F32)


def _peer_step(blk, xnT_ref, dn_ref, upT_ref, rk_ref, cn_ref, e1_ref, e2_ref, o_ref, h_ref,
               a_out, a_in):
    nch = PEER_CHUNKS_PER_STEP
    per_piece = nch // PEER_UP_PIECES
    assert nch % PEER_UP_PIECES == 0 and PEER_UP_PIECES % (o_ref.shape[1] // PEER_LANE_TILE) == 0
    hc = PEER_HIDDEN_CHUNKS
    assert nch % hc == 0
    if a_out is not None:
        _peer_hidden(0, xnT_ref, dn_ref, h_ref)
    for c in range(nch):
        if a_out is not None and c % hc == 0 and c + hc < nch:
            _peer_hidden(c // hc + 1, xnT_ref, dn_ref, h_ref)
        if a_in is not None and c % per_piece == PEER_UP_PHASE:
            _peer_up_piece(c // per_piece, upT_ref, a_in, o_ref)
        if a_out is not None:
            _peer_gate_chunk(blk, c, h_ref, rk_ref, cn_ref, e1_ref, e2_ref, a_out)


def _peer_body(xnT_ref, dn_ref, upT_ref, rk_ref, cn_ref, e1_ref, e2_ref, o_ref, a0_ref, a1_ref,
               h_ref):
    j = pl.program_id(1)
    nblk = PEER_N // PEER_EB
    step = functools.partial(_peer_step, j, xnT_ref, dn_ref, upT_ref, rk_ref, cn_ref, e1_ref,
                             e2_ref, o_ref, h_ref)
    even = lax.rem(j, 2) == 0
    inner = jnp.logical_and(j > 0, j < nblk)

    @pl.when(j == 0)
    def _():
        o_ref[...] = jnp.zeros_like(o_ref)
        step(a0_ref, None)

    @pl.when(jnp.logical_and(inner, even))
    def _():
        step(a0_ref, a1_ref)

    @pl.when(jnp.logical_and(inner, jnp.logical_not(even)))
    def _():
        step(a1_ref, a0_ref)

    @pl.when(j == nblk)
    def _():
        step(None, a1_ref if (nblk - 1) % 2 else a0_ref)


def _peer(xnT, down_b, upT_b, rk, cn, e1, e2):
    t = xnT.shape[1]
    tm = TM_PEER
    nblk = PEER_N // PEER_EB
    hk = (PEER_HEADS, PEER_NKEYS, tm)
    hk_slab = (PEER_HEADS, tm // PEER_SLAB_LANES, PEER_NKEYS, PEER_SLAB_LANES)
    tok3 = lambda i, j: (0, 0, i)
    slab = lambda i, j: (0, i, 0, 0)
    return pl.pallas_call(
        _peer_body,
        grid=(t // tm, nblk + 1),
        in_specs=[pl.BlockSpec((D_MODEL, tm), lambda i, j: (0, i)),
                  pl.BlockSpec((PEER_EB, D_MODEL), lambda i, j: (jnp.minimum(j, nblk - 1), 0)),
                  pl.BlockSpec((D_MODEL, PEER_EB), lambda i, j: (0, jnp.maximum(j - 1, 0))),
                  pl.BlockSpec(hk, tok3), pl.BlockSpec(hk_slab, slab),
                  pl.BlockSpec(hk_slab, slab), pl.BlockSpec(hk, tok3)],
        out_specs=pl.BlockSpec((D_MODEL, tm), lambda i, j: (0, i)),
        out_shape=jax.ShapeDtypeStruct((D_MODEL, t), F32),
        scratch_shapes=[pltpu.VMEM((PEER_EB, tm), BF16), pltpu.VMEM((PEER_EB, tm), BF16),
                        pltpu.VMEM((PEER_EB, tm), F32)],
        compiler_params=_params("parallel", "arbitrary", vmem_limit_bytes=PEER_VMEM_LIMIT_BYTES),
        name="peer",
    )(xnT, down_b, upT_b, rk, cn, e1, e2)


def _final_body(x2_ref, pT_ref, g_ref, o_ref):
    o_ref[...] = _rms(x2_ref[...] + pT_ref[...].T, g_ref[...])


def _final(x2, peerT, g):
    t = x2.shape[0]
    tm = TM_FINAL
    row = lambda i: (i, 0)
    return pl.pallas_call(
        _final_body,
        grid=(t // tm,),
        in_specs=[pl.BlockSpec((tm, D_MODEL), row), pl.BlockSpec((D_MODEL, tm), lambda i: (0, i)),
                  pl.BlockSpec((1, D_MODEL), lambda i: (0, 0))],
        out_specs=pl.BlockSpec((tm, D_MODEL), row),
        out_shape=jax.ShapeDtypeStruct((t, D_MODEL), F32),
        compiler_params=_params("parallel"),
        name="final",
    )(x2, peerT, g)


def kernel(x, norm_mix_g, w_in, conv_w, norm_conv_out_g, norm_attn_out_g, w_out, norm_ffn_g,
           peer_w_query, peer_sub_keys1, peer_sub_keys2, peer_down, peer_up, final_norm_g):
    batch, seq, d = x.shape
    assert w_in.shape[0] == 1, "one layer: the residual stream between layers is not implemented"
    assert d == D_MODEL and seq % MOBA_BLOCK == 0 and seq % TM_MIX == 0 and seq % TM_INPROJ == 0
    assert seq // MOBA_BLOCK > MOBA_TOPK
    x2d = x.reshape(batch * seq, d)
    bgate, u, qT, k, vT = _inproj(x2d, norm_mix_g[0].reshape(1, D_MODEL), w_in[0].astype(BF16),
                                  batch, seq)
    yT = _moba(qT, k, vT)
    x2, xnT = _mix(x2d, bgate, u, yT, conv_w[0], norm_conv_out_g[0].reshape(1, D_CONV),
                   norm_attn_out_g[0].reshape(1, D_ATTN), w_out[0].astype(BF16),
                   norm_ffn_g[0].reshape(1, D_MODEL), seq)
    rk, cn, e1, e2 = _prep(xnT, peer_w_query[0].T.astype(BF16),
                           peer_sub_keys1[0].astype(BF16), peer_sub_keys2[0].astype(BF16))
    outT = _peer(xnT, peer_down[0].astype(BF16), peer_up[0].T.astype(BF16), rk, cn, e1, e2)
    out = _final(x2, outT, final_norm_g.reshape(1, D_MODEL))
    return out.reshape(batch, seq, d)
```

```python
import functools

import numpy as np
import jax
import jax.numpy as jnp
from jax import lax
from jax.experimental import pallas as pl
from jax.experimental.pallas import tpu as pltpu

F32 = jnp.float32
BF16 = jnp.bfloat16

D_MODEL = 1024
D_CONV = 512
D_ATTN = 512
D_PROJ = 3 * D_CONV + 3 * D_ATTN
CONV_WIDTH = 3
ATTN_HEADS = 8
HEAD_DIM = 64
MOBA_BLOCK = 256
MOBA_TOPK = 3
MOBA_HEADS_PER_STEP = 2
PEER_HEADS = 8
PEER_NKEYS = 128
PEER_N = PEER_NKEYS * PEER_NKEYS
PEER_DKEY = 256
PEER_HALF = PEER_DKEY // 2
PEER_TOPK = 16
EPS = 1e-6
NEG = -1e30
GELU_HALF = 0.5

VMEM_LIMIT_BYTES = 48 * 1024 * 1024
PEER_VMEM_LIMIT_BYTES = 56 * 1024 * 1024
SUBLANES = 8

TM_INPROJ = 512
TM_MIX = 512
TM_PREP = 256
TM_PEER = 512
PEER_ROWS_PER_CHUNK = 1
PEER_CHUNK = PEER_ROWS_PER_CHUNK * PEER_NKEYS
PEER_CHUNKS_PER_STEP = 16
PEER_EB = PEER_CHUNKS_PER_STEP * PEER_CHUNK
BF16_ROWS = 16
PEER_LANE_TILE = 256
PEER_GATE_LANES = 256
PEER_SLAB_LANES = 128
PEER_ROWS_PER_PASS = 1
PEER_SLABS_PER_PASS = 8
PEER_UP_PIECES = 4
PEER_HIDDEN_CHUNKS = 1
PEER_HIDDEN_LANES = 512
PEER_UP_PHASE = 0

_CAND_PAIRS = [(r1, r2) for r1 in range(PEER_TOPK) for r2 in range(PEER_TOPK)
               if (r1 + 1) * (r2 + 1) <= PEER_TOPK]


def _params(*sem, vmem_limit_bytes=VMEM_LIMIT_BYTES):
    return pltpu.CompilerParams(dimension_semantics=sem, vmem_limit_bytes=vmem_limit_bytes)


def _rms(x, g):
    return x * lax.rsqrt(jnp.mean(x * x, axis=-1, keepdims=True) + EPS) * g


def _inproj_body(x_ref, g_ref, w_ref, b_ref, u_ref, q_ref, k_ref, v_ref):
    h = _rms(x_ref[...], g_ref[...]).astype(BF16)
    proj = jnp.dot(h, w_ref[...], preferred_element_type=F32)
    b_ref[...] = proj[:, 0:D_CONV]
    u_ref[...] = proj[:, D_CONV:2 * D_CONV] * proj[:, 2 * D_CONV:3 * D_CONV]
    o = 3 * D_CONV
    q_ref[...] = proj[:, o:o + D_ATTN].T.astype(BF16)
    k_ref[...] = proj[:, o + D_ATTN:o + 2 * D_ATTN].astype(BF16)
    v_ref[...] = proj[:, o + 2 * D_ATTN:o + 3 * D_ATTN].T.astype(BF16)


def _inproj(x2d, g, w_in_b, batch, seq):
    t = x2d.shape[0]
    tm = TM_INPROJ
    tps = seq // tm
    row = lambda i: (i, 0)
    fixed = lambda i: (0, 0)
    chan = lambda i: (i // tps, 0, i % tps)
    return pl.pallas_call(
        _inproj_body,
        grid=(t // tm,),
        in_specs=[pl.BlockSpec((tm, D_MODEL), row),
                  pl.BlockSpec((1, D_MODEL), fixed),
                  pl.BlockSpec((D_MODEL, D_PROJ), fixed)],
        out_specs=[pl.BlockSpec((tm, D_CONV), row), pl.BlockSpec((tm, D_CONV), row),
                   pl.BlockSpec((None, D_ATTN, tm), chan), pl.BlockSpec((tm, D_ATTN), row),
                   pl.BlockSpec((None, D_ATTN, tm), chan)],
        out_shape=[jax.ShapeDtypeStruct((t, D_CONV), F32), jax.ShapeDtypeStruct((t, D_CONV), F32),
                   jax.ShapeDtypeStruct((batch, D_ATTN, seq), BF16),
                   jax.ShapeDtypeStruct((t, D_ATTN), BF16),
                   jax.ShapeDtypeStruct((batch, D_ATTN, seq), BF16)],
        compiler_params=_params("parallel"),
        name="inproj",
    )(x2d, g, w_in_b)


def _moba_head(qT, k_ref, vT_ref, o_ref, rows, seq):
    nb = seq // MOBA_BLOCK
    bs = MOBA_BLOCK
    scale = float(1.0 / np.sqrt(HEAD_DIM))
    blk = lax.broadcasted_iota(jnp.int32, (nb, seq), 0)
    own = lax.shift_right_logical(lax.broadcasted_iota(jnp.int32, (nb, seq), 1),
                                  int(np.log2(bs)))
    avg = jnp.where(own == blk, 1.0 / bs, 0.0).astype(BF16)
    kmean = jnp.dot(avg, k_ref[...], preferred_element_type=F32)
    gate = jnp.dot(kmean.astype(BF16), qT, preferred_element_type=F32)
    past = blk < own
    g1 = jnp.where(past, gate, NEG)
    thr = g1
    for _ in range(MOBA_TOPK - 1):
        m = jnp.max(thr, axis=0, keepdims=True)
        thr = jnp.where(thr == m, NEG, thr)
    thr = jnp.max(thr, axis=0, keepdims=True)
    sel = jnp.where(jnp.logical_and(past, g1 >= thr), 1.0, 0.0)

    kpos = lax.broadcasted_iota(jnp.int32, (bs, bs), 0)
    qpos = lax.broadcasted_iota(jnp.int32, (bs, bs), 1)
    causal = kpos <= qpos
    for i in range(nb):
        n = (i + 1) * bs
        lanes = slice(i * bs, (i + 1) * bs)
        sT = jnp.dot(k_ref[0:n, :], qT[:, lanes], preferred_element_type=F32) * scale
        pieces = []
        for j in range(i):
            keep = sel[j:j + 1, lanes] > 0.5
            pieces.append(jnp.where(keep, sT[j * bs:(j + 1) * bs], NEG))
        pieces.append(jnp.where(causal, sT[i * bs:n], NEG))
        s = pieces[0] if i == 0 else jnp.concatenate(pieces, axis=0)
        m = jnp.max(s, axis=0, keepdims=True)
        p = jnp.exp(s - m)
        l = jnp.sum(p, axis=0, keepdims=True)
        oT = jnp.dot(vT_ref[rows, 0:n], p.astype(BF16), preferred_element_type=F32)
        o_ref[rows, lanes] = oT / l


def _moba_body(qT_ref, k_ref, vT_ref, o_ref, *, seq):
    chan = lax.broadcasted_iota(jnp.int32, (MOBA_HEADS_PER_STEP * HEAD_DIM, seq), 0)
    q2 = qT_ref[...]
    for hh in range(MOBA_HEADS_PER_STEP):
        rows = slice(hh * HEAD_DIM, (hh + 1) * HEAD_DIM)
        mine = jnp.logical_and(chan >= rows.start, chan < rows.stop)
        _moba_head(jnp.where(mine, q2, jnp.zeros((), BF16)), k_ref, vT_ref, o_ref, rows, seq)


def _moba(qT, k, vT):
    b, d, s = qT.shape
    w = MOBA_HEADS_PER_STEP * HEAD_DIM
    chan = lambda i, j: (i, j, 0)
    return pl.pallas_call(
        functools.partial(_moba_body, seq=s),
        grid=(b, d // w),
        in_specs=[pl.BlockSpec((None, w, s), chan),
                  pl.BlockSpec((s, w), lambda i, j: (i, j)),
                  pl.BlockSpec((None, w, s), chan)],
        out_specs=pl.BlockSpec((None, w, s), chan),
        out_shape=jax.ShapeDtypeStruct((b, d, s), F32),
        compiler_params=_params("parallel", "parallel"),
        name="moba",
    )(qT, k, vT)


def _mix_body(x_ref, b_ref, u_ref, uh_ref, ya_ref, cw_ref, gc_ref, ga_ref, wo_ref, gf_ref,
              x2_ref, xn_ref, uext_ref, *, tiles_per_seq):
    tm = u_ref.shape[0]
    first = (pl.program_id(0) % tiles_per_seq) == 0
    uext_ref[0:SUBLANES, :] = jnp.where(first, 0.0, uh_ref[...])
    u = u_ref[...]
    uext_ref[SUBLANES:SUBLANES + tm, :] = u
    cw = cw_ref[...]
    y = (cw[2:3, :] * u
         + cw[1:2, :] * uext_ref[SUBLANES - 1:SUBLANES - 1 + tm, :]
         + cw[0:1, :] * uext_ref[SUBLANES - 2:SUBLANES - 2 + tm, :])
    rc = _rms(b_ref[...] * y, gc_ref[...]).astype(BF16)
    ra = _rms(ya_ref[...].T, ga_ref[...]).astype(BF16)
    mix = (jnp.dot(rc, wo_ref[0:D_CONV, :], preferred_element_type=F32)
           + jnp.dot(ra, wo_ref[D_CONV:D_CONV + D_ATTN, :], preferred_element_type=F32))
    x2 = x_ref[...] + mix
    x2_ref[...] = x2
    xn_ref[...] = _rms(x2, gf_ref[...]).T.astype(BF16)


def _mix(x2d, bgate, u, yattnT, conv_w, gc, ga, w_out_b, gf, seq):
    t = x2d.shape[0]
    tm = TM_MIX
    tps = seq // tm
    row = lambda i: (i, 0)
    fixed = lambda i: (0, 0)
    halo = lambda i: (jnp.maximum(i * (tm // SUBLANES) - 1, 0), 0)
    return pl.pallas_call(
        functools.partial(_mix_body, tiles_per_seq=tps),
        grid=(t // tm,),
        in_specs=[pl.BlockSpec((tm, D_MODEL), row),
                  pl.BlockSpec((tm, D_CONV), row),
                  pl.BlockSpec((tm, D_CONV), row),
                  pl.BlockSpec((SUBLANES, D_CONV), halo),
                  pl.BlockSpec((None, D_ATTN, tm), lambda i: (i // tps, 0, i % tps)),
                  pl.BlockSpec((CONV_WIDTH, D_CONV), fixed),
                  pl.BlockSpec((1, D_CONV), fixed),
                  pl.BlockSpec((1, D_ATTN), fixed),
                  pl.BlockSpec((D_MODEL, D_MODEL), fixed),
                  pl.BlockSpec((1, D_MODEL), fixed)],
        out_specs=[pl.BlockSpec((tm, D_MODEL), row), pl.BlockSpec((D_MODEL, tm), lambda i: (0, i))],
        out_shape=[jax.ShapeDtypeStruct((t, D_MODEL), F32),
                   jax.ShapeDtypeStruct((D_MODEL, t), BF16)],
        scratch_shapes=[pltpu.VMEM((tm + SUBLANES, D_CONV), F32)],
        compiler_params=_params("parallel"),
        name="mix",
    )(x2d, bgate, u, u, yattnT, conv_w, gc, ga, w_out_b, gf)


def _oddeven_merge_sort_network(n):
    pairs = []
    p = 1
    while p < n:
        k = p
        while k >= 1:
            for j in range(k % p, n - k, 2 * k):
                for i in range(min(k, n - j - k)):
                    if (i + j) // (2 * p) == (i + j + k) // (2 * p):
                        pairs.append((i + j, i + j + k))
            k //= 2
        p *= 2
    return pairs


_SORT_NET = _oddeven_merge_sort_network(PEER_TOPK)


def _exchange(g, i, j):
    g[i], g[j] = jnp.maximum(g[i], g[j]), jnp.minimum(g[i], g[j])


def _slabs(s):
    return [s[SUBLANES * i:SUBLANES * (i + 1), :] for i in range(s.shape[0] // SUBLANES)]


def _sorted_top(slabs):
    n = PEER_TOPK
    g = list(slabs)
    assert len(g) == n
    for i, j in _SORT_NET:
        _exchange(g, i, j)
    shift = SUBLANES // 2
    while shift >= 1:
        rolled = [pltpu.roll(x, shift, axis=0) for x in g]
        g = [jnp.maximum(g[i], rolled[n - 1 - i]) for i in range(n)]
        d = n // 2
        while d >= 1:
            for i in range(n):
                if i & d == 0:
                    _exchange(g, i, i + d)
            d //= 2
        shift //= 2
    return g


def _kth_largest_packed(cands, k):
    sub = lax.broadcasted_iota(jnp.int32, cands[0].shape, 0)
    packed = []
    for q in range(0, len(cands), SUBLANES):
        group = cands[q:q + SUBLANES]
        x = group[0] if len(group) == SUBLANES else jnp.full_like(cands[0], NEG)
        for p, cnd in enumerate(group):
            if p or len(group) < SUBLANES:
                x = jnp.where(sub == p, cnd, x)
        packed.append(x)
    out = []
    for _ in range(k):
        m = functools.reduce(jnp.maximum, packed)
        m = jnp.max(m, axis=0, keepdims=True)
        out.append(jnp.broadcast_to(m, cands[0].shape))
        packed = [jnp.where(x == m, NEG, x) for x in packed]
    return out


def _prep_head(s1, s2):
    last = PEER_TOPK - 1
    g1, g2 = _slabs(s1), _slabs(s2)
    v1, v2 = _sorted_top(g1), _sorted_top(g2)
    cands = [v1[r1] + v2[r2] for r1, r2 in _CAND_PAIRS]
    tops = _kth_largest_packed(cands, PEER_TOPK)
    tau = tops[last]
    z = functools.reduce(jnp.add, [jnp.exp(t - tops[0]) for t in tops])
    cs = [jnp.zeros_like(tau) for _ in range(PEER_TOPK)]
    for cnd, (r1, _) in zip(cands, _CAND_PAIRS):
        cs[r1] = cs[r1] + jnp.where(cnd >= tau, 1.0, 0.0)
    scale = GELU_HALF / z
    rank2, cnt, e1, e2 = [], [], [], []
    for a, b in zip(g1, g2):
        c = jnp.zeros_like(a)
        for r in range(PEER_TOPK):
            c = jnp.where(a == v1[r], cs[r], c)
        cnt.append(c)
        rk = jnp.full_like(b, float(PEER_TOPK))
        for r in range(last, -1, -1):
            rk = jnp.where(b >= v2[r], float(r), rk)
        rank2.append(rk)
        e1.append(jnp.where(a >= v1[last], jnp.exp(a - v1[0]), 0.0) * scale)
        e2.append(jnp.where(b >= v2[last], jnp.exp(b - v2[0]), 0.0))
    cat = lambda parts: jnp.concatenate(parts, axis=0)
    return cat(rank2), cat(cnt), cat(e1), cat(e2)


def _prep_body(xnT_ref, wqT_ref, k1_ref, k2_ref, rk_ref, cn_ref, e1_ref, e2_ref, qry_ref):
    qry_ref[...] = jnp.dot(wqT_ref[...], xnT_ref[...], preferred_element_type=F32)
    for h in range(PEER_HEADS):
        o = h * PEER_DKEY
        q1 = qry_ref[o:o + PEER_HALF, :].astype(BF16)
        q2 = qry_ref[o + PEER_HALF:o + PEER_DKEY, :].astype(BF16)
        s1 = jnp.dot(k1_ref[h], q1, preferred_element_type=F32)
        s2 = jnp.dot(k2_ref[h], q2, preferred_element_type=F32)
        rank2, cnt, e1, e2 = _prep_head(s1, s2)
        rk_ref[h] = rank2.astype(BF16)
        e2_ref[h] = e2.astype(BF16)
        for lt in range(s1.shape[1] // PEER_SLAB_LANES):
            lanes = slice(lt * PEER_SLAB_LANES, (lt + 1) * PEER_SLAB_LANES)
            cn_ref[h, lt] = cnt[:, lanes]
            e1_ref[h, lt] = e1[:, lanes]


def _prep(xnT, wqT_b, k1_b, k2_b):
    t = xnT.shape[1]
    tm = TM_PREP
    col = lambda i: (0, i)
    col3 = lambda i: (0, 0, i)
    slab = lambda i: (0, i, 0, 0)
    hk = (PEER_HEADS, PEER_NKEYS, tm)
    hk_slab = (PEER_HEADS, tm // PEER_SLAB_LANES, PEER_NKEYS, PEER_SLAB_LANES)
    slab_shape = jax.ShapeDtypeStruct((PEER_HEADS, t // PEER_SLAB_LANES, PEER_NKEYS, PEER_SLAB_LANES), F32)
    return pl.pallas_call(
        _prep_body,
        grid=(t // tm,),
        in_specs=[pl.BlockSpec((D_MODEL, tm), col),
                  pl.BlockSpec((PEER_HEADS * PEER_DKEY, D_MODEL), lambda i: (0, 0)),
                  pl.BlockSpec((PEER_HEADS, PEER_NKEYS, PEER_HALF), lambda i: (0, 0, 0)),
                  pl.BlockSpec((PEER_HEADS, PEER_NKEYS, PEER_HALF), lambda i: (0, 0, 0))],
        out_specs=[pl.BlockSpec(hk, col3), pl.BlockSpec(hk_slab, slab), pl.BlockSpec(hk_slab, slab),
                   pl.BlockSpec(hk, lambda i: (0, 0, jnp.bitwise_xor(i, 1)))],
        out_shape=[jax.ShapeDtypeStruct((PEER_HEADS, PEER_NKEYS, t), BF16), slab_shape, slab_shape,
                   jax.ShapeDtypeStruct((PEER_HEADS, PEER_NKEYS, t), BF16)],
        scratch_shapes=[pltpu.VMEM((PEER_HEADS * PEER_DKEY, tm), F32)],
        compiler_params=_params("parallel"),
        name="prep",
    )(xnT, wqT_b, k1_b, k2_b)


def _peer_hidden(k, xnT_ref, dn_ref, h_ref):
    n = PEER_HIDDEN_CHUNKS * PEER_CHUNK
    rows = slice(k * n, (k + 1) * n)
    for lt in range(h_ref.shape[1] // PEER_HIDDEN_LANES):
        lanes = slice(lt * PEER_HIDDEN_LANES, (lt + 1) * PEER_HIDDEN_LANES)
        h_ref[rows, lanes] = jnp.dot(dn_ref[rows, :], xnT_ref[:, lanes], preferred_element_type=F32)


def _swapped_lanes(lt):
    assert TM_PREP == PEER_GATE_LANES and (TM_PEER // PEER_GATE_LANES) % 2 == 0
    return slice((lt ^ 1) * PEER_GATE_LANES, ((lt ^ 1) + 1) * PEER_GATE_LANES)


def _row_tile(ref, h, lt, i1):
    per = PEER_GATE_LANES // PEER_SLAB_LANES
    parts = [ref[h, lt * per + k, pl.ds(i1, SUBLANES, stride=0), :] for k in range(per)]
    x = jnp.concatenate(parts, axis=1)
    return jnp.concatenate([x] * (BF16_ROWS // SUBLANES), axis=0).astype(BF16)


def _peer_gate_chunk(blk, c, h_ref, rk_ref, cn_ref, e1_ref, e2_ref, a_ref):
    tm = a_ref.shape[1]
    sqrt_half = float(np.sqrt(0.5))
    slabs = PEER_NKEYS // BF16_ROWS
    zero = jnp.zeros((), BF16)
    row0 = (blk * PEER_CHUNKS_PER_STEP + c) * PEER_ROWS_PER_CHUNK
    for g in range(0, PEER_ROWS_PER_CHUNK, PEER_ROWS_PER_PASS):
        rows = range(g, g + PEER_ROWS_PER_PASS)
        act = {}
        for r in rows:
            hr = h_ref[c * PEER_CHUNK + r * PEER_NKEYS:c * PEER_CHUNK + (r + 1) * PEER_NKEYS, :]
            act[r] = (hr * (1.0 + lax.erf(hr * sqrt_half))).astype(BF16)
        for lt in range(tm // PEER_GATE_LANES):
            lanes = slice(lt * PEER_GATE_LANES, (lt + 1) * PEER_GATE_LANES)
            for s0 in range(0, slabs, PEER_SLABS_PER_PASS):
                w = {(r, s): None for r in rows for s in range(s0, s0 + PEER_SLABS_PER_PASS)}
                for h in range(PEER_HEADS):
                    c16 = {r: _row_tile(cn_ref, h, lt, row0 + r) for r in rows}
                    e16 = {r: _row_tile(e1_ref, h, lt, row0 + r) for r in rows}
                    for s in range(s0, s0 + PEER_SLABS_PER_PASS):
                        sl = slice(s * BF16_ROWS, (s + 1) * BF16_ROWS)
                        rk = rk_ref[h, sl, lanes]
                        e2 = e2_ref[h, sl, _swapped_lanes(lt)]
                        for r in rows:
                            term = jnp.where(rk < c16[r], e2 * e16[r], zero)
                            w[r, s] = term if w[r, s] is None else w[r, s] + term
                for (r, s), ws in w.items():
                    sl = slice(s * BF16_ROWS, (s + 1) * BF16_ROWS)
                    base = c * PEER_CHUNK + r * PEER_NKEYS
                    a_ref[base + s * BF16_ROWS:base + (s + 1) * BF16_ROWS, lanes] = ws * act[r][sl, lanes]


def _peer_up_piece(p, upT_ref, a_ref, o_ref):
    lane_tiles = o_ref.shape[1] // PEER_LANE_TILE
    band = o_ref.shape[0] // (PEER_UP_PIECES // lane_tiles)
    rows = slice((p // lane_tiles) * band, (p // lane_tiles + 1) * band)
    lanes = slice((p % lane_tiles) * PEER_LANE_TILE, (p % lane_tiles + 1) * PEER_LANE_TILE)
    o_ref[rows, lanes] += jnp.dot(upT_ref[rows, :], a_ref[:, lanes], preferred_element_type=F32)


def _peer_step(blk, xnT_ref, dn_ref, upT_ref, rk_ref, cn_ref, e1_ref, e2_ref, o_ref, h_ref,
               a_out, a_in):
    nch = PEER_CHUNKS_PER_STEP
    per_piece = nch // PEER_UP_PIECES
    assert nch % PEER_UP_PIECES == 0 and PEER_UP_PIECES % (o_ref.shape[1] // PEER_LANE_TILE) == 0
    hc = PEER_HIDDEN_CHUNKS
    assert nch % hc == 0
    if a_out is not None:
        _peer_hidden(0, xnT_ref, dn_ref, h_ref)
    for c in range(nch):
        if a_out is not None and c % hc == 0 and c + hc < nch:
            _peer_hidden(c // hc + 1, xnT_ref, dn_ref, h_ref)
        if a_in is not None and c % per_piece == PEER_UP_PHASE:
            _peer_up_piece(c // per_piece, upT_ref, a_in, o_ref)
        if a_out is not None:
            _peer_gate_chunk(blk, c, h_ref, rk_ref, cn_ref, e1_ref, e2_ref, a_out)


def _peer_body(xnT_ref, dn_ref, upT_ref, rk_ref, cn_ref, e1_ref, e2_ref, x2_ref, g_ref, out_ref,
               a0_ref, a1_ref, h_ref, o_ref):
    j = pl.program_id(1)
    nblk = PEER_N // PEER_EB
    step = functools.partial(_peer_step, j, xnT_ref, dn_ref, upT_ref, rk_ref, cn_ref, e1_ref,
                             e2_ref, o_ref, h_ref)
    even = lax.rem(j, 2) == 0
    inner = jnp.logical_and(j > 0, j < nblk)

    @pl.when(j == 0)
    def _():
        o_ref[...] = jnp.zeros_like(o_ref)
        step(a0_ref, None)

    @pl.when(jnp.logical_and(inner, even))
    def _():
        step(a0_ref, a1_ref)

    @pl.when(jnp.logical_and(inner, jnp.logical_not(even)))
    def _():
        step(a1_ref, a0_ref)

    @pl.when(j == nblk)
    def _():
        step(None, a1_ref if (nblk - 1) % 2 else a0_ref)
        out_ref[...] = _rms(x2_ref[...] + o_ref[...].T, g_ref[...])


def _peer(xnT, down_b, upT_b, rk, cn, e1, e2, x2, final_g):
    t = xnT.shape[1]
    tm = TM_PEER
    nblk = PEER_N // PEER_EB
    hk = (PEER_HEADS, PEER_NKEYS, tm)
    hk_slab = (PEER_HEADS, tm // PEER_SLAB_LANES, PEER_NKEYS, PEER_SLAB_LANES)
    tok3 = lambda i, j: (0, 0, i)
    slab = lambda i, j: (0, i, 0, 0)
    return pl.pallas_call(
        _peer_body,
        grid=(t // tm, nblk + 1),
        in_specs=[pl.BlockSpec((D_MODEL, tm), lambda i, j: (0, i)),
                  pl.BlockSpec((PEER_EB, D_MODEL), lambda i, j: (jnp.minimum(j, nblk - 1), 0)),
                  pl.BlockSpec((D_MODEL, PEER_EB), lambda i, j: (0, jnp.maximum(j - 1, 0))),
                  pl.BlockSpec(hk, tok3), pl.BlockSpec(hk_slab, slab),
                  pl.BlockSpec(hk_slab, slab), pl.BlockSpec(hk, tok3),
                  pl.BlockSpec((tm, D_MODEL), lambda i, j: (i, 0)),
                  pl.BlockSpec((1, D_MODEL), lambda i, j: (0, 0))],
        out_specs=pl.BlockSpec((tm, D_MODEL), lambda i, j: (i, 0)),
        out_shape=jax.ShapeDtypeStruct((t, D_MODEL), F32),
        scratch_shapes=[pltpu.VMEM((PEER_EB, tm), BF16), pltpu.VMEM((PEER_EB, tm), BF16),
                        pltpu.VMEM((PEER_EB, tm), F32), pltpu.VMEM((D_MODEL, tm), F32)],
        compiler_params=_params("parallel", "arbitrary", vmem_limit_bytes=PEER_VMEM_LIMIT_BYTES),
        name="peer",
    )(xnT, down_b, upT_b, rk, cn, e1, e2, x2, final_g)


def kernel(x, norm_mix_g, w_in, conv_w, norm_conv_out_g, norm_attn_out_g, w_out, norm_ffn_g,
           peer_w_query, peer_sub_keys1, peer_sub_keys2, peer_down, peer_up, final_norm_g):
    batch, seq, d = x.shape
    assert w_in.shape[0] == 1, "one layer: the residual stream between layers is not implemented"
    assert d == D_MODEL and seq % MOBA_BLOCK == 0 and seq % TM_MIX == 0 and seq % TM_INPROJ == 0
    assert seq // MOBA_BLOCK > MOBA_TOPK
    x2d = x.reshape(batch * seq, d)
    bgate, u, qT, k, vT = _inproj(x2d, norm_mix_g[0].reshape(1, D_MODEL), w_in[0].astype(BF16),
                                  batch, seq)
    yT = _moba(qT, k, vT)
    x2, xnT = _mix(x2d, bgate, u, yT, conv_w[0], norm_conv_out_g[0].reshape(1, D_CONV),
                   norm_attn_out_g[0].reshape(1, D_ATTN), w_out[0].astype(BF16),
                   norm_ffn_g[0].reshape(1, D_MODEL), seq)
    rk, cn, e1, e2 = _prep(xnT, peer_w_query[0].T.astype(BF16),
                           peer_sub_keys1[0].astype(BF16), peer_sub_keys2[0].astype(BF16))
    out = _peer(xnT, peer_down[0].astype(BF16), peer_up[0].T.astype(BF16), rk, cn, e1, e2,
                x2, final_norm_g.reshape(1, D_MODEL))
    return out.reshape(batch, seq, d)
```

```python
import functools

import numpy as np
import jax
import jax.numpy as jnp
from jax import lax
from jax.experimental import pallas as pl
from jax.experimental.pallas import tpu as pltpu

F32 = jnp.float32
BF16 = jnp.bfloat16

D_MODEL = 1024
D_CONV = 512
D_ATTN = 512
D_PROJ = 3 * D_CONV + 3 * D_ATTN
CONV_WIDTH = 3
ATTN_HEADS = 8
HEAD_DIM = 64
MOBA_BLOCK = 256
MOBA_TOPK = 3
MOBA_HEADS_PER_STEP = 2
PEER_HEADS = 8
PEER_NKEYS = 128
PEER_N = PEER_NKEYS * PEER_NKEYS
PEER_DKEY = 256
PEER_HALF = PEER_DKEY // 2
PEER_TOPK = 16
EPS = 1e-6
NEG = -1e30
GELU_HALF = 0.5

VMEM_LIMIT_BYTES = 48 * 1024 * 1024
PEER_VMEM_LIMIT_BYTES = 56 * 1024 * 1024
SUBLANES = 8

TM_INPROJ = 512
TM_MIX = 512
TM_PREP = 256
TM_PEER = 512
PEER_ROWS_PER_CHUNK = 1
PEER_CHUNK = PEER_ROWS_PER_CHUNK * PEER_NKEYS
PEER_CHUNKS_PER_STEP = 16
PEER_EB = PEER_CHUNKS_PER_STEP * PEER_CHUNK
BF16_ROWS = 16
PEER_LANE_TILE = 256
PEER_GATE_LANES = 256
PEER_SLAB_LANES = 128
PEER_ROWS_PER_PASS = 1
PEER_SLABS_PER_PASS = 8
PEER_UP_PIECES = 4
PEER_HIDDEN_CHUNKS = 1
PEER_HIDDEN_LANES = 512
PEER_UP_PHASE = 0

_CAND_PAIRS = [(r1, r2) for r1 in range(PEER_TOPK) for r2 in range(PEER_TOPK)
               if (r1 + 1) * (r2 + 1) <= PEER_TOPK]


def _params(*sem, vmem_limit_bytes=VMEM_LIMIT_BYTES):
    return pltpu.CompilerParams(dimension_semantics=sem, vmem_limit_bytes=vmem_limit_bytes)


def _rms(x, g):
    return x * lax.rsqrt(jnp.mean(x * x, axis=-1, keepdims=True) + EPS) * g


def _inproj_body(x_ref, g_ref, w_ref, b_ref, u_ref, q_ref, k_ref, v_ref):
    h = _rms(x_ref[...], g_ref[...]).astype(BF16)
    proj = jnp.dot(h, w_ref[...], preferred_element_type=F32)
    b_ref[...] = proj[:, 0:D_CONV]
    u_ref[...] = proj[:, D_CONV:2 * D_CONV] * proj[:, 2 * D_CONV:3 * D_CONV]
    o = 3 * D_CONV
    q_ref[...] = proj[:, o:o + D_ATTN].T.astype(BF16)
    k_ref[...] = proj[:, o + D_ATTN:o + 2 * D_ATTN].astype(BF16)
    v_ref[...] = proj[:, o + 2 * D_ATTN:o + 3 * D_ATTN].T.astype(BF16)


def _inproj(x2d, g, w_in_b, batch, seq):
    t = x2d.shape[0]
    tm = TM_INPROJ
    tps = seq // tm
    row = lambda i: (i, 0)
    fixed = lambda i: (0, 0)
    chan = lambda i: (i // tps, 0, i % tps)
    return pl.pallas_call(
        _inproj_body,
        grid=(t // tm,),
        in_specs=[pl.BlockSpec((tm, D_MODEL), row),
                  pl.BlockSpec((1, D_MODEL), fixed),
                  pl.BlockSpec((D_MODEL, D_PROJ), fixed)],
        out_specs=[pl.BlockSpec((tm, D_CONV), row), pl.BlockSpec((tm, D_CONV), row),
                   pl.BlockSpec((None, D_ATTN, tm), chan), pl.BlockSpec((tm, D_ATTN), row),
                   pl.BlockSpec((None, D_ATTN, tm), chan)],
        out_shape=[jax.ShapeDtypeStruct((t, D_CONV), F32), jax.ShapeDtypeStruct((t, D_CONV), F32),
                   jax.ShapeDtypeStruct((batch, D_ATTN, seq), BF16),
                   jax.ShapeDtypeStruct((t, D_ATTN), BF16),
                   jax.ShapeDtypeStruct((batch, D_ATTN, seq), BF16)],
        compiler_params=_params("parallel"),
        name="inproj",
    )(x2d, g, w_in_b)


def _moba_head(qT, k_ref, vT_ref, o_ref, rows, seq):
    nb = seq // MOBA_BLOCK
    bs = MOBA_BLOCK
    scale = float(1.0 / np.sqrt(HEAD_DIM))
    blk = lax.broadcasted_iota(jnp.int32, (nb, seq), 0)
    own = lax.shift_right_logical(lax.broadcasted_iota(jnp.int32, (nb, seq), 1),
                                  int(np.log2(bs)))
    avg = jnp.where(own == blk, 1.0 / bs, 0.0).astype(BF16)
    kmean = jnp.dot(avg, k_ref[...], preferred_element_type=F32)
    gate = jnp.dot(kmean.astype(BF16), qT, preferred_element_type=F32)
    past = blk < own
    g1 = jnp.where(past, gate, NEG)
    thr = g1
    for _ in range(MOBA_TOPK - 1):
        m = jnp.max(thr, axis=0, keepdims=True)
        thr = jnp.where(thr == m, NEG, thr)
    thr = jnp.max(thr, axis=0, keepdims=True)
    sel = jnp.where(jnp.logical_and(past, g1 >= thr), 1.0, 0.0)

    kpos = lax.broadcasted_iota(jnp.int32, (bs, bs), 0)
    qpos = lax.broadcasted_iota(jnp.int32, (bs, bs), 1)
    causal = kpos <= qpos
    for i in range(nb):
        n = (i + 1) * bs
        lanes = slice(i * bs, (i + 1) * bs)
        sT = jnp.dot(k_ref[0:n, :], qT[:, lanes], preferred_element_type=F32) * scale
        pieces = []
        for j in range(i):
            keep = sel[j:j + 1, lanes] > 0.5
            pieces.append(jnp.where(keep, sT[j * bs:(j + 1) * bs], NEG))
        pieces.append(jnp.where(causal, sT[i * bs:n], NEG))
        s = pieces[0] if i == 0 else jnp.concatenate(pieces, axis=0)
        m = jnp.max(s, axis=0, keepdims=True)
        p = jnp.exp(s - m)
        l = jnp.sum(p, axis=0, keepdims=True)
        oT = jnp.dot(vT_ref[rows, 0:n], p.astype(BF16), preferred_element_type=F32)
        o_ref[rows, lanes] = oT / l


def _moba_body(qT_ref, k_ref, vT_ref, o_ref, *, seq):
    chan = lax.broadcasted_iota(jnp.int32, (MOBA_HEADS_PER_STEP * HEAD_DIM, seq), 0)
    q2 = qT_ref[...]
    for hh in range(MOBA_HEADS_PER_STEP):
        rows = slice(hh * HEAD_DIM, (hh + 1) * HEAD_DIM)
        mine = jnp.logical_and(chan >= rows.start, chan < rows.stop)
        _moba_head(jnp.where(mine, q2, jnp.zeros((), BF16)), k_ref, vT_ref, o_ref, rows, seq)


def _moba(qT, k, vT):
    b, d, s = qT.shape
    w = MOBA_HEADS_PER_STEP * HEAD_DIM
    chan = lambda i, j: (i, j, 0)
    return pl.pallas_call(
        functools.partial(_moba_body, seq=s),
        grid=(b, d // w),
        in_specs=[pl.BlockSpec((None, w, s), chan),
                  pl.BlockSpec((s, w), lambda i, j: (i, j)),
                  pl.BlockSpec((None, w, s), chan)],
        out_specs=pl.BlockSpec((None, w, s), chan),
        out_shape=jax.ShapeDtypeStruct((b, d, s), F32),
        compiler_params=_params("parallel", "parallel"),
        name="moba",
    )(qT, k, vT)


def _mix_body(x_ref, b_ref, u_ref, uh_ref, ya_ref, cw_ref, gc_ref, ga_ref, wo_ref, gf_ref,
              x2_ref, xn_ref, uext_ref, *, tiles_per_seq):
    tm = u_ref.shape[0]
    first = (pl.program_id(0) % tiles_per_seq) == 0
    uext_ref[0:SUBLANES, :] = jnp.where(first, 0.0, uh_ref[...])
    u = u_ref[...]
    uext_ref[SUBLANES:SUBLANES + tm, :] = u
    cw = cw_ref[...]
    y = (cw[2:3, :] * u
         + cw[1:2, :] * uext_ref[SUBLANES - 1:SUBLANES - 1 + tm, :]
         + cw[0:1, :] * uext_ref[SUBLANES - 2:SUBLANES - 2 + tm, :])
    rc = _rms(b_ref[...] * y, gc_ref[...]).astype(BF16)
    ra = _rms(ya_ref[...].T, ga_ref[...]).astype(BF16)
    mix = (jnp.dot(rc, wo_ref[0:D_CONV, :], preferred_element_type=F32)
           + jnp.dot(ra, wo_ref[D_CONV:D_CONV + D_ATTN, :], preferred_element_type=F32))
    x2 = x_ref[...] + mix
    x2_ref[...] = x2
    xn_ref[...] = _rms(x2, gf_ref[...]).T.astype(BF16)


def _mix(x2d, bgate, u, yattnT, conv_w, gc, ga, w_out_b, gf, seq):
    t = x2d.shape[0]
    tm = TM_MIX
    tps = seq // tm
    row = lambda i: (i, 0)
    fixed = lambda i: (0, 0)
    halo = lambda i: (jnp.maximum(i * (tm // SUBLANES) - 1, 0), 0)
    return pl.pallas_call(
        functools.partial(_mix_body, tiles_per_seq=tps),
        grid=(t // tm,),
        in_specs=[pl.BlockSpec((tm, D_MODEL), row),
                  pl.BlockSpec((tm, D_CONV), row),
                  pl.BlockSpec((tm, D_CONV), row),
                  pl.BlockSpec((SUBLANES, D_CONV), halo),
                  pl.BlockSpec((None, D_ATTN, tm), lambda i: (i // tps, 0, i % tps)),
                  pl.BlockSpec((CONV_WIDTH, D_CONV), fixed),
                  pl.BlockSpec((1, D_CONV), fixed),
                  pl.BlockSpec((1, D_ATTN), fixed),
                  pl.BlockSpec((D_MODEL, D_MODEL), fixed),
                  pl.BlockSpec((1, D_MODEL), fixed)],
        out_specs=[pl.BlockSpec((tm, D_MODEL), row), pl.BlockSpec((D_MODEL, tm), lambda i: (0, i))],
        out_shape=[jax.ShapeDtypeStruct((t, D_MODEL), F32),
                   jax.ShapeDtypeStruct((D_MODEL, t), BF16)],
        scratch_shapes=[pltpu.VMEM((tm + SUBLANES, D_CONV), F32)],
        compiler_params=_params("parallel"),
        name="mix",
    )(x2d, bgate, u, u, yattnT, conv_w, gc, ga, w_out_b, gf)


def _oddeven_merge_sort_network(n):
    pairs = []
    p = 1
    while p < n:
        k = p
        while k >= 1:
            for j in range(k % p, n - k, 2 * k):
                for i in range(min(k, n - j - k)):
                    if (i + j) // (2 * p) == (i + j + k) // (2 * p):
                        pairs.append((i + j, i + j + k))
            k //= 2
        p *= 2
    return pairs


_SORT_NET = _oddeven_merge_sort_network(PEER_TOPK)


def _exchange(g, i, j):
    g[i], g[j] = jnp.maximum(g[i], g[j]), jnp.minimum(g[i], g[j])


def _slabs(s):
    return [s[SUBLANES * i:SUBLANES * (i + 1), :] for i in range(s.shape[0] // SUBLANES)]


def _sorted_top(slabs):
    n = PEER_TOPK
    g = list(slabs)
    assert len(g) == n
    for i, j in _SORT_NET:
        _exchange(g, i, j)
    shift = SUBLANES // 2
    while shift >= 1:
        rolled = [pltpu.roll(x, shift, axis=0) for x in g]
        g = [jnp.maximum(g[i], rolled[n - 1 - i]) for i in range(n)]
        d = n // 2
        while d >= 1:
            for i in range(n):
                if i & d == 0:
                    _exchange(g, i, i + d)
            d //= 2
        shift //= 2
    return g


def _kth_largest_packed(cands, k):
    sub = lax.broadcasted_iota(jnp.int32, cands[0].shape, 0)
    packed = []
    for q in range(0, len(cands), SUBLANES):
        group = cands[q:q + SUBLANES]
        x = group[0] if len(group) == SUBLANES else jnp.full_like(cands[0], NEG)
        for p, cnd in enumerate(group):
            if p or len(group) < SUBLANES:
                x = jnp.where(sub == p, cnd, x)
        packed.append(x)
    out = []
    for _ in range(k):
        m = functools.reduce(jnp.maximum, packed)
        m = jnp.max(m, axis=0, keepdims=True)
        out.append(jnp.broadcast_to(m, cands[0].shape))
        packed = [jnp.where(x == m, NEG, x) for x in packed]
    return out


def _prep_head(s1, s2):
    last = PEER_TOPK - 1
    g1, g2 = _slabs(s1), _slabs(s2)
    v1, v2 = _sorted_top(g1), _sorted_top(g2)
    cands = [v1[r1] + v2[r2] for r1, r2 in _CAND_PAIRS]
    tops = _kth_largest_packed(cands, PEER_TOPK)
    tau = tops[last]
    z = functools.reduce(jnp.add, [jnp.exp(t - tops[0]) for t in tops])
    cs = [jnp.zeros_like(tau) for _ in range(PEER_TOPK)]
    for cnd, (r1, _) in zip(cands, _CAND_PAIRS):
        cs[r1] = cs[r1] + jnp.where(cnd >= tau, 1.0, 0.0)
    scale = GELU_HALF / z
    rank2, cnt, e1, e2 = [], [], [], []
    for a, b in zip(g1, g2):
        c = jnp.zeros_like(a)
        for r in range(PEER_TOPK):
            c = jnp.where(a == v1[r], cs[r], c)
        cnt.append(c)
        rk = jnp.full_like(b, float(PEER_TOPK))
        for r in range(last, -1, -1):
            rk = jnp.where(b >= v2[r], float(r), rk)
        rank2.append(rk)
        e1.append(jnp.where(a >= v1[last], jnp.exp(a - v1[0]), 0.0) * scale)
        e2.append(jnp.where(b >= v2[last], jnp.exp(b - v2[0]), 0.0))
    cat = lambda parts: jnp.concatenate(parts, axis=0)
    return cat(rank2), cat(cnt), cat(e1), cat(e2)


def _prep_body(xnT_ref, wqT_ref, k1_ref, k2_ref, rk_ref, cn_ref, e1_ref, e2_ref, qry_ref):
    qry_ref[...] = jnp.dot(wqT_ref[...], xnT_ref[...], preferred_element_type=F32)
    for h in range(PEER_HEADS):
        o = h * PEER_DKEY
        q1 = qry_ref[o:o + PEER_HALF, :].astype(BF16)
        q2 = qry_ref[o + PEER_HALF:o + PEER_DKEY, :].astype(BF16)
        s1 = jnp.dot(k1_ref[h], q1, preferred_element_type=F32)
        s2 = jnp.dot(k2_ref[h], q2, preferred_element_type=F32)
        rank2, cnt, e1, e2 = _prep_head(s1, s2)
        rk_ref[h] = rank2.astype(BF16)
        e2_ref[h] = e2.astype(BF16)
        for lt in range(s1.shape[1] // PEER_SLAB_LANES):
            lanes = slice(lt * PEER_SLAB_LANES, (lt + 1) * PEER_SLAB_LANES)
            cn_ref[h, lt] = cnt[:, lanes]
            e1_ref[h, lt] = e1[:, lanes]


def _prep(xnT, wqT_b, k1_b, k2_b):
    t = xnT.shape[1]
    tm = TM_PREP
    col = lambda i: (0, i)
    col3 = lambda i: (0, 0, i)
    slab = lambda i: (0, i, 0, 0)
    hk = (PEER_HEADS, PEER_NKEYS, tm)
    hk_slab = (PEER_HEADS, tm // PEER_SLAB_LANES, PEER_NKEYS, PEER_SLAB_LANES)
    slab_shape = jax.ShapeDtypeStruct((PEER_HEADS, t // PEER_SLAB_LANES, PEER_NKEYS, PEER_SLAB_LANES), F32)
    return pl.pallas_call(
        _prep_body,
        grid=(t // tm,),
        in_specs=[pl.BlockSpec((D_MODEL, tm), col),
                  pl.BlockSpec((PEER_HEADS * PEER_DKEY, D_MODEL), lambda i: (0, 0)),
                  pl.BlockSpec((PEER_HEADS, PEER_NKEYS, PEER_HALF), lambda i: (0, 0, 0)),
                  pl.BlockSpec((PEER_HEADS, PEER_NKEYS, PEER_HALF), lambda i: (0, 0, 0))],
        out_specs=[pl.BlockSpec(hk, col3), pl.BlockSpec(hk_slab, slab), pl.BlockSpec(hk_slab, slab),
                   pl.BlockSpec(hk, lambda i: (0, 0, jnp.bitwise_xor(i, 1)))],
        out_shape=[jax.ShapeDtypeStruct((PEER_HEADS, PEER_NKEYS, t), BF16), slab_shape, slab_shape,
                   jax.ShapeDtypeStruct((PEER_HEADS, PEER_NKEYS, t), BF16)],
        scratch_shapes=[pltpu.VMEM((PEER_HEADS * PEER_DKEY, tm), F32)],
        compiler_params=_params("parallel"),
        name="prep",
    )(xnT, wqT_b, k1_b, k2_b)


def _peer_hidden(k, xnT_ref, dn_ref, h_ref):
    n = PEER_HIDDEN_CHUNKS * PEER_CHUNK
    rows = slice(k * n, (k + 1) * n)
    for lt in range(h_ref.shape[1] // PEER_HIDDEN_LANES):
        lanes = slice(lt * PEER_HIDDEN_LANES, (lt + 1) * PEER_HIDDEN_LANES)
        h_ref[rows, lanes] = jnp.dot(dn_ref[rows, :], xnT_ref[:, lanes], preferred_element_type=F32)


def _swapped_lanes(lt):
    assert TM_PREP == PEER_GATE_LANES and (TM_PEER // PEER_GATE_LANES) % 2 == 0
    return slice((lt ^ 1) * PEER_GATE_LANES, ((lt ^ 1) + 1) * PEER_GATE_LANES)


def _row_tile(ref, h, lt, i1):
    per = PEER_GATE_LANES // PEER_SLAB_LANES
    parts = [ref[h, lt * per + k, pl.ds(i1, SUBLANES, stride=0), :] for k in range(per)]
    x = jnp.concatenate(parts, axis=1)
    return jnp.concatenate([x] * (BF16_ROWS // SUBLANES), axis=0).astype(BF16)


def _peer_gate_chunk(blk, c, h_ref, rk_ref, cn_ref, e1_ref, e2_ref, a_ref):
    tm = a_ref.shape[1]
    sqrt_half = float(np.sqrt(0.5))
    slabs = PEER_NKEYS // BF16_ROWS
    zero = jnp.zeros((), BF16)
    row0 = (blk * PEER_CHUNKS_PER_STEP + c) * PEER_ROWS_PER_CHUNK
    for g in range(0, PEER_ROWS_PER_CHUNK, PEER_ROWS_PER_PASS):
        rows = range(g, g + PEER_ROWS_PER_PASS)
        act = {}
        for r in rows:
            hr = h_ref[c * PEER_CHUNK + r * PEER_NKEYS:c * PEER_CHUNK + (r + 1) * PEER_NKEYS, :]
            act[r] = (hr * (1.0 + lax.erf(hr * sqrt_half))).astype(BF16)
        for lt in range(tm // PEER_GATE_LANES):
            lanes = slice(lt * PEER_GATE_LANES, (lt + 1) * PEER_GATE_LANES)
            for s0 in range(0, slabs, PEER_SLABS_PER_PASS):
                w = {(r, s): None for r in rows for s in range(s0, s0 + PEER_SLABS_PER_PASS)}
                for h in range(PEER_HEADS):
                    c16 = {r: _row_tile(cn_ref, h, lt, row0 + r) for r in rows}
                    e16 = {r: _row_tile(e1_ref, h, lt, row0 + r) for r in rows}
                    for s in range(s0, s0 + PEER_SLABS_PER_PASS):
                        sl = slice(s * BF16_ROWS, (s + 1) * BF16_ROWS)
                        rk = rk_ref[h, sl, lanes]
                        e2 = e2_ref[h, sl, _swapped_lanes(lt)]
                        for r in rows:
                            term = jnp.where(rk < c16[r], e2 * e16[r], zero)
                            w[r, s] = term if w[r, s] is None else w[r, s] + term
                for (r, s), ws in w.items():
                    sl = slice(s * BF16_ROWS, (s + 1) * BF16_ROWS)
                    base = c * PEER_CHUNK + r * PEER_NKEYS
                    a_ref[base + s * BF16_ROWS:base + (s + 1) * BF16_ROWS, lanes] = ws * act[r][sl, lanes]


def _peer_up_piece(p, upT_ref, a_ref, o_ref):
    lane_tiles = o_ref.shape[1] // PEER_LANE_TILE
    band = o_ref.shape[0] // (PEER_UP_PIECES // lane_tiles)
    rows = slice((p // lane_tiles) * band, (p // lane_tiles + 1) * band)
    lanes = slice((p % lane_tiles) * PEER_LANE_TILE, (p % lane_tiles + 1) * PEER_LANE_TILE)
    o_ref[rows, lanes] += jnp.dot(upT_ref[rows, :], a_ref[:, lanes], preferred_element_type=F32)


def _peer_step(blk, xnT_ref, dn_ref, upT_ref, rk_ref, cn_ref, e1_ref, e2_ref, o_ref, h_ref,
               a_out, a_in):
    nch = PEER_CHUNKS_PER_STEP
    per_piece = nch // PEER_UP_PIECES
    assert nch % PEER_UP_PIECES == 0 and PEER_UP_PIECES % (o_ref.shape[1] // PEER_LANE_TILE) == 0
    hc = PEER_HIDDEN_CHUNKS
    assert nch % hc == 0
    if a_out is not None:
        _peer_hidden(0, xnT_ref, dn_ref, h_ref)
    for c in range(nch):
        if a_in is not None and c % per_piece == PEER_UP_PHASE:
            _peer_up_piece(c // per_piece, upT_ref, a_in, o_ref)
        if a_out is not None and c % hc == 0 and c + hc < nch:
            _peer_hidden(c // hc + 1, xnT_ref, dn_ref, h_ref)
        if a_out is not None:
            _peer_gate_chunk(blk, c, h_ref, rk_ref, cn_ref, e1_ref, e2_ref, a_out)


def _peer_body(xnT_ref, dn_ref, upT_ref, rk_ref, cn_ref, e1_ref, e2_ref, x2_ref, g_ref, out_ref,
               a0_ref, a1_ref, h_ref, o_ref):
    j = pl.program_id(1)
    nblk = PEER_N // PEER_EB
    step = functools.partial(_peer_step, j, xnT_ref, dn_ref, upT_ref, rk_ref, cn_ref, e1_ref,
                             e2_ref, o_ref, h_ref)
    even = lax.rem(j, 2) == 0
    inner = jnp.logical_and(j > 0, j < nblk)

    @pl.when(j == 0)
    def _():
        o_ref[...] = jnp.zeros_like(o_ref)
        step(a0_ref, None)

    @pl.when(jnp.logical_and(inner, even))
    def _():
        step(a0_ref, a1_ref)

    @pl.when(jnp.logical_and(inner, jnp.logical_not(even)))
    def _():
        step(a1_ref, a0_ref)

    @pl.when(j == nblk)
    def _():
        step(None, a1_ref if (nblk - 1) % 2 else a0_ref)
        out_ref[...] = _rms(x2_ref[...] + o_ref[...].T, g_ref[...])


def _peer(xnT, down_b, upT_b, rk, cn, e1, e2, x2, final_g):
    t = xnT.shape[1]
    tm = TM_PEER
    nblk = PEER_N // PEER_EB
    hk = (PEER_HEADS, PEER_NKEYS, tm)
    hk_slab = (PEER_HEADS, tm // PEER_SLAB_LANES, PEER_NKEYS, PEER_SLAB_LANES)
    tok3 = lambda i, j: (0, 0, i)
    slab = lambda i, j: (0, i, 0, 0)
    return pl.pallas_call(
        _peer_body,
        grid=(t // tm, nblk + 1),
        in_specs=[pl.BlockSpec((D_MODEL, tm), lambda i, j: (0, i)),
                  pl.BlockSpec((PEER_EB, D_MODEL), lambda i, j: (jnp.minimum(j, nblk - 1), 0)),
                  pl.BlockSpec((D_MODEL, PEER_EB), lambda i, j: (0, jnp.maximum(j - 1, 0))),
                  pl.BlockSpec(hk, tok3), pl.BlockSpec(hk_slab, slab),
                  pl.BlockSpec(hk_slab, slab), pl.BlockSpec(hk, tok3),
                  pl.BlockSpec((tm, D_MODEL), lambda i, j: (i, 0)),
                  pl.BlockSpec((1, D_MODEL), lambda i, j: (0, 0))],
        out_specs=pl.BlockSpec((tm, D_MODEL), lambda i, j: (i, 0)),
        out_shape=jax.ShapeDtypeStruct((t, D_MODEL), F32),
        scratch_shapes=[pltpu.VMEM((PEER_EB, tm), BF16), pltpu.VMEM((PEER_EB, tm), BF16),
                        pltpu.VMEM((PEER_EB, tm), F32), pltpu.VMEM((D_MODEL, tm), F32)],
        compiler_params=_params("parallel", "arbitrary", vmem_limit_bytes=PEER_VMEM_LIMIT_BYTES),
        name="peer",
    )(xnT, down_b, upT_b, rk, cn, e1, e2, x2, final_g)


def kernel(x, norm_mix_g, w_in, conv_w, norm_conv_out_g, norm_attn_out_g, w_out, norm_ffn_g,
           peer_w_query, peer_sub_keys1, peer_sub_keys2, peer_down, peer_up, final_norm_g):
    batch, seq, d = x.shape
    assert w_in.shape[0] == 1, "one layer: the residual stream between layers is not implemented"
    assert d == D_MODEL and seq % MOBA_BLOCK == 0 and seq % TM_MIX == 0 and seq % TM_INPROJ == 0
    assert seq // MOBA_BLOCK > MOBA_TOPK
    x2d = x.reshape(batch * seq, d)
    bgate, u, qT, k, vT = _inproj(x2d, norm_mix_g[0].reshape(1, D_MODEL), w_in[0].astype(BF16),
                                  batch, seq)
    yT = _moba(qT, k, vT)
    x2, xnT = _mix(x2d, bgate, u, yT, conv_w[0], norm_conv_out_g[0].reshape(1, D_CONV),
                   norm_attn_out_g[0].reshape(1, D_ATTN), w_out[0].astype(BF16),
                   norm_ffn_g[0].reshape(1, D_MODEL), seq)
    rk, cn, e1, e2 = _prep(xnT, peer_w_query[0].T.astype(BF16),
                           peer_sub_keys1[0].astype(BF16), peer_sub_keys2[0].astype(BF16))
    out = _peer(xnT, peer_down[0].astype(BF16), peer_up[0].T.astype(BF16), rk, cn, e1, e2,
                x2, final_norm_g.reshape(1, D_MODEL))
    return out.reshape(batch, seq, d)
```
